```python
import jax, jax.numpy as jnp
from jax import lax
import numpy as np

D_MODEL = 4096
BATCH = 2
SEQ = 4096
DEPTH = 2

CTX_LEN = 256
GRID_W = 64
N_MIXERS = 2
N_FOURIER = (DEPTH + 1) // 2
N_MLA = DEPTH // 2
EPS = 1e-6
ADA_CHUNKS = 6

F_GROUPS = 8
F_GROUP_DIM = D_MODEL // F_GROUPS

MLA_HEADS = 32
QK_NOPE = 128
QK_ROPE = 64
V_DIM = 128
Q_LORA = 1024
KV_LORA = 512
MLA_SCALE = (QK_NOPE + QK_ROPE) ** -0.5
ROPE_THETA = 10000.0
Q_BLOCK = 128

PEER_HEADS = 8
PEER_QDIM = 256
PEER_HALF = PEER_QDIM // 2
N_KEYS = 128
N_EXPERTS = N_KEYS * N_KEYS
PEER_TOPK = 16
PEER_BLOCK = 128

kernel_name = 'hybrid_fnet_mla_peer_diffusion_block'


def rms_norm(x, g):
    xf = x.astype(jnp.float32)
    y = xf * lax.rsqrt(jnp.mean(xf * xf, axis=-1, keepdims=True) + EPS)
    return (y * g.astype(jnp.float32)).astype(x.dtype)


def modulate(h, shift, scale):
    return h * (1 + scale) + shift


def axial_rope_tables(n_tokens, dtype):
    rows = n_tokens // GRID_W
    row_id = jnp.repeat(jnp.arange(rows), GRID_W).astype(jnp.float32)
    col_id = jnp.tile(jnp.arange(GRID_W), rows).astype(jnp.float32)
    axis_dim = QK_ROPE // 2
    inv_freq = ROPE_THETA ** (-jnp.arange(0, axis_dim, 2, dtype=jnp.float32) / axis_dim)
    ang = jnp.stack([row_id[:, None] * inv_freq, col_id[:, None] * inv_freq], axis=-2)
    return jnp.cos(ang).astype(dtype), jnp.sin(ang).astype(dtype)


def apply_axial_rope(x, cos, sin):
    xs = x.reshape(x.shape[:-1] + (2, 2, QK_ROPE // 4))
    x1, x2 = xs[..., 0, :], xs[..., 1, :]
    out = jnp.stack([x1 * cos - x2 * sin, x1 * sin + x2 * cos], axis=-2)
    return out.reshape(x.shape)


def fourier_mix(h, w_o):
    b, n, d = h.shape
    hg = h.astype(jnp.float32).reshape(b, n, F_GROUPS, F_GROUP_DIM)
    f = jnp.fft.fft2(hg, axes=(1, 3), norm='ortho').real
    return f.reshape(b, n, d).astype(h.dtype) @ w_o


def mla_qkv(h, w_in, q_g, w_uq, kv_g, w_ukv):
    b, n, _ = h.shape
    z = h @ w_in
    cq = z[..., :Q_LORA]
    ckv = z[..., Q_LORA:Q_LORA + KV_LORA]
    k_pe = z[..., Q_LORA + KV_LORA:]
    q = (rms_norm(cq, q_g) @ w_uq).reshape(b, n, MLA_HEADS, QK_NOPE + QK_ROPE)
    kv = (rms_norm(ckv, kv_g) @ w_ukv).reshape(b, n, MLA_HEADS, QK_NOPE + V_DIM)
    return q[..., :QK_NOPE], q[..., QK_NOPE:], kv[..., :QK_NOPE], k_pe, kv[..., QK_NOPE:]


def mla_attend(q_nope, q_pe, k_nope, k_pe, v):
    s = jnp.einsum('bqhd,bkhd->bhqk', q_nope, k_nope) + jnp.einsum('bqhr,bkr->bhqk', q_pe, k_pe)
    p = jax.nn.softmax(s.astype(jnp.float32) * MLA_SCALE, axis=-1).astype(v.dtype)
    return jnp.einsum('bhqk,bkhd->bqhd', p, v)


def mla_mixer(h_lat, h_ctx, w_in, q_g, w_uq, kv_g, w_ukv, w_o, cos, sin, with_ctx_out):
    b, n, _ = h_lat.shape
    qn_l, qp_l, kn_l, kp_l, v_l = mla_qkv(h_lat, w_in, q_g, w_uq, kv_g, w_ukv)
    qp_l = apply_axial_rope(qp_l, cos[:, None], sin[:, None])
    kp_l = apply_axial_rope(kp_l, cos, sin)
    qn_c, qp_c, kn_c, kp_c, v_c = mla_qkv(h_ctx, w_in, q_g, w_uq, kv_g, w_ukv)
    kn = jnp.concatenate([kn_c, kn_l], axis=1)
    kp = jnp.concatenate([kp_c, kp_l], axis=1)
    vv = jnp.concatenate([v_c, v_l], axis=1)
    nb = n // Q_BLOCK

    def to_blocks(t):
        return jnp.moveaxis(t.reshape((b, nb, Q_BLOCK) + t.shape[2:]), 1, 0)

    o = lax.map(lambda qs: mla_attend(qs[0], qs[1], kn, kp, vv), (to_blocks(qn_l), to_blocks(qp_l)))
    y_lat = jnp.moveaxis(o, 0, 1).reshape(b, n, MLA_HEADS * V_DIM) @ w_o
    if with_ctx_out:
        o_c = mla_attend(qn_c, qp_c, kn_c, kp_c, v_c)
        y_ctx = o_c.reshape(b, h_ctx.shape[1], MLA_HEADS * V_DIM) @ w_o
        return y_lat, y_ctx
    return y_lat, None


def peer(h, wq, k1, k2, u, v):
    t = h.shape[0]
    q = (h @ wq).reshape(t, PEER_HEADS, 2, PEER_HALF)
    s1 = jnp.einsum('thd,hnd->thn', q[:, :, 0], k1).astype(jnp.float32)
    s2 = jnp.einsum('thd,hnd->thn', q[:, :, 1], k2).astype(jnp.float32)
    v1, i1 = lax.top_k(s1, PEER_TOPK)
    v2, i2 = lax.top_k(s2, PEER_TOPK)
    cand = (v1[..., :, None] + v2[..., None, :]).reshape(t, PEER_HEADS, PEER_TOPK * PEER_TOPK)
    cidx = (i1[..., :, None] * N_KEYS + i2[..., None, :]).reshape(t, PEER_HEADS, PEER_TOPK * PEER_TOPK)
    best, pos = lax.top_k(cand, PEER_TOPK)
    eidx = jnp.take_along_axis(cidx, pos, axis=-1)
    gate = jax.nn.softmax(best, axis=-1).astype(h.dtype)
    nb = t // PEER_BLOCK

    def expert_block(args):
        hb, eb, gb = args
        ub = jnp.take(u, eb, axis=0)
        a = jnp.einsum('td,thkd->thk', hb, ub)
        vb = jnp.take(v, eb, axis=0)
        return jnp.einsum('thk,thkd->td', gb * jax.nn.gelu(a, approximate=False), vb)

    y = lax.map(expert_block, (h.reshape(nb, PEER_BLOCK, -1),
                               eidx.reshape(nb, PEER_BLOCK, PEER_HEADS, PEER_TOPK),
                               gate.reshape(nb, PEER_BLOCK, PEER_HEADS, PEER_TOPK)))
    return y.reshape(t, -1)


def setup_inputs(seed: int = 0) -> dict:
    key = jax.random.key(seed)
    ks = jax.random.split(key, 21)
    f32 = jnp.float32
    D = D_MODEL

    def nrm(k, shape, scale):
        return jax.random.normal(k, shape, f32) * scale

    def gain(k, shape):
        return 1.0 + 0.05 * jax.random.normal(k, shape, f32)

    return {
        'x': nrm(ks[0], (BATCH, SEQ, D), 1.0),
        'c': nrm(ks[1], (BATCH, D), 1.0),
        'ctx': nrm(ks[2], (BATCH, CTX_LEN, D), 1.0),
        'c_ctx': nrm(ks[3], (D,), 1.0),
        'ada_w': nrm(ks[4], (DEPTH, D, ADA_CHUNKS * D), 0.5 * D ** -0.5),
        'ada_b': nrm(ks[5], (DEPTH, ADA_CHUNKS * D), 0.02),
        'norm1_g': gain(ks[6], (DEPTH, D)),
        'norm2_g': gain(ks[7], (DEPTH, D)),
        'final_g': gain(ks[8], (D,)),
        'fnet_wo': nrm(ks[9], (N_FOURIER, D, D), D ** -0.5),
        'mla_w_in': nrm(ks[10], (N_MLA, D, Q_LORA + KV_LORA + QK_ROPE), D ** -0.5),
        'mla_q_g': gain(ks[11], (N_MLA, Q_LORA)),
        'mla_w_uq': nrm(ks[12], (N_MLA, Q_LORA, MLA_HEADS * (QK_NOPE + QK_ROPE)), Q_LORA ** -0.5),
        'mla_kv_g': gain(ks[13], (N_MLA, KV_LORA)),
        'mla_w_ukv': nrm(ks[14], (N_MLA, KV_LORA, MLA_HEADS * (QK_NOPE + V_DIM)), KV_LORA ** -0.5),
        'mla_wo': nrm(ks[15], (N_MLA, MLA_HEADS * V_DIM, D), (MLA_HEADS * V_DIM) ** -0.5),
        'peer_wq': nrm(ks[16], (DEPTH, D, PEER_HEADS * PEER_QDIM), D ** -0.5),
        'peer_k1': nrm(ks[17], (DEPTH, PEER_HEADS, N_KEYS, PEER_HALF), PEER_HALF ** -0.5),
        'peer_k2': nrm(ks[18], (DEPTH, PEER_HEADS, N_KEYS, PEER_HALF), PEER_HALF ** -0.5),
        'peer_u': nrm(ks[19], (DEPTH, N_EXPERTS, D), D ** -0.5),
        'peer_v': nrm(ks[20], (DEPTH, N_EXPERTS, D), 1.0),
    }


def reference(x, c, ctx, c_ctx, ada_w, ada_b, norm1_g, norm2_g, final_g, fnet_wo,
              mla_w_in, mla_q_g, mla_w_uq, mla_kv_g, mla_w_ukv, mla_wo,
              peer_wq, peer_k1, peer_k2, peer_u, peer_v):
    b, n, d = x.shape
    n_ctx = ctx.shape[1]
    cos, sin = axial_rope_tables(n, x.dtype)
    cs = ctx
    for i in range(DEPTH):
        last = i == DEPTH - 1
        mod_l = (jax.nn.silu(c) @ ada_w[i] + ada_b[i]).reshape(b, ADA_CHUNKS, 1, d)
        mod_c = (jax.nn.silu(c_ctx)[None] @ ada_w[i] + ada_b[i]).reshape(1, ADA_CHUNKS, 1, d)
        sh1_l, sc1_l, g1_l, sh2_l, sc2_l, g2_l = [mod_l[:, j] for j in range(ADA_CHUNKS)]
        sh1_c, sc1_c, g1_c, sh2_c, sc2_c, g2_c = [mod_c[:, j] for j in range(ADA_CHUNKS)]

        h_l = modulate(rms_norm(x, norm1_g[i]), sh1_l, sc1_l)
        h_c = modulate(rms_norm(cs, norm1_g[i]), sh1_c, sc1_c)
        j = i // N_MIXERS
        if i % N_MIXERS == 0:
            y_l = fourier_mix(h_l, fnet_wo[j])
            y_c = None if last else fourier_mix(h_c, fnet_wo[j])
        else:
            y_l, y_c = mla_mixer(h_l, h_c, mla_w_in[j], mla_q_g[j], mla_w_uq[j], mla_kv_g[j],
                                 mla_w_ukv[j], mla_wo[j], cos, sin, not last)
        x = x + g1_l * y_l
        if not last:
            cs = cs + g1_c * y_c

        h2_l = modulate(rms_norm(x, norm2_g[i]), sh2_l, sc2_l).reshape(b * n, d)
        if last:
            f_l = peer(h2_l, peer_wq[i], peer_k1[i], peer_k2[i], peer_u[i], peer_v[i])
        else:
            h2_c = modulate(rms_norm(cs, norm2_g[i]), sh2_c, sc2_c).reshape(b * n_ctx, d)
            f = peer(jnp.concatenate([h2_l, h2_c], axis=0),
                     peer_wq[i], peer_k1[i], peer_k2[i], peer_u[i], peer_v[i])
            f_l = f[:b * n]
            cs = cs + g2_c * f[b * n:].reshape(b, n_ctx, d)
        x = x + g2_l * f_l.reshape(b, n, d)
    return rms_norm(x, final_g)
```

```python
import functools
import math

import numpy as np
import jax
import jax.numpy as jnp
from jax import lax
from jax.experimental import pallas as pl
from jax.experimental.pallas import tpu as pltpu

F32 = jnp.float32
BF16 = jnp.bfloat16

EPS = 1e-6
ADA_CHUNKS = 6
F_GROUPS = 8
GRID_W = 64
MLA_HEADS = 32
QK_NOPE = 128
QK_ROPE = 64
V_DIM = 128
Q_LORA = 1024
KV_LORA = 512
ROPE_THETA = 10000.0
PEER_HEADS = 8
PEER_HALF = 128
N_KEYS = 128
PEER_TOPK = 16

LANES = 128
SUBLANES = 8
VMEM_LIMIT = 56 * 1024 * 1024
NEG_INF = float("-inf")


def _params(*sem):
    return pltpu.CompilerParams(dimension_semantics=sem, vmem_limit_bytes=VMEM_LIMIT)


def _ada_kernel(ct_ref, w_ref, b_ref, o_ref, sb_ref, *, kc, rows):
    d, bn = w_ref.shape
    nt = bn // LANES

    @pl.when(pl.program_id(0) == 0)
    def _():
        c = ct_ref[...]
        s = c * jax.nn.sigmoid(c)
        for r in range(rows):
            sb_ref[r] = jnp.broadcast_to(s[:, r:r + 1], (d, LANES))

    def body(i, accs):
        k0 = pl.multiple_of(i * kc, kc)
        out = list(accs)
        for t in range(nt):
            wv = w_ref[pl.ds(k0, kc), t * LANES:(t + 1) * LANES]
            for r in range(rows):
                p = wv * sb_ref[r, pl.ds(k0, kc), :]
                out[r * nt + t] = out[r * nt + t] + p.reshape(kc // SUBLANES, SUBLANES, LANES).sum(axis=0)
        return tuple(out)

    init = tuple(jnp.zeros((SUBLANES, LANES), F32) for _ in range(rows * nt))
    accs = lax.fori_loop(0, d // kc, body, init)
    o_ref[...] = jnp.zeros_like(o_ref)
    for r in range(rows):
        for t in range(nt):
            row = accs[r * nt + t].sum(axis=0, keepdims=True) + b_ref[:, t * LANES:(t + 1) * LANES]
            o_ref[r:r + 1, t * LANES:(t + 1) * LANES] = row


def ada_linear(ct, w, b, *, rows, bn=512, kc=256):
    d, n = w.shape
    bn = min(bn, n)
    kc = min(kc, d)
    return pl.pallas_call(
        functools.partial(_ada_kernel, kc=kc, rows=rows),
        grid=(n // bn,),
        in_specs=[
            pl.BlockSpec((d, SUBLANES), lambda j: (0, 0)),
            pl.BlockSpec((d, bn), lambda j: (0, j)),
            pl.BlockSpec((1, bn), lambda j: (0, j)),
        ],
        out_specs=pl.BlockSpec((SUBLANES, bn), lambda j: (0, j)),
        out_shape=jax.ShapeDtypeStruct((SUBLANES, n), F32),
        scratch_shapes=[pltpu.VMEM((rows, d, LANES), F32)],
        compiler_params=_params("arbitrary"),
        name="ada_linear",
    )(ct, w, b.reshape(1, n))


def _norm_kernel(*refs, has_res, has_mod, write_x):
    it = iter(refs)
    x_ref = next(it)
    if has_res:
        y_ref, gate_ref = next(it), next(it)
    g_ref = next(it)
    if has_mod:
        shift_ref, scale_ref = next(it), next(it)
    if write_x:
        xo_ref = next(it)
    h_ref = next(it)

    x = x_ref[...]
    if has_res:
        x = x + gate_ref[...] * y_ref[...].astype(F32)
    if write_x:
        xo_ref[...] = x
    y = x * lax.rsqrt(jnp.mean(x * x, axis=-1, keepdims=True) + EPS)
    y = y * g_ref[...]
    if has_mod:
        y = y * (1.0 + scale_ref[...]) + shift_ref[...]
    h_ref[...] = y.astype(h_ref.dtype)


def norm_block(x, g, sel, *, res=None, mod=None, write_x=False, out_dtype=BF16, bm=256):
    t, d = x.shape
    bm = min(bm, t)
    row = pl.BlockSpec((bm, d), lambda i: (i, 0))

    def mod_spec(a):
        return pl.BlockSpec((None, None, 1, d), lambda i, a=a: (sel(i, bm), a, 0, 0))

    args, specs = [x], [row]
    if res is not None:
        args += [res[0], res[1]]
        specs += [row, mod_spec(res[2])]
    args.append(g.reshape(1, d))
    specs.append(pl.BlockSpec((1, d), lambda i: (0, 0)))
    if mod is not None:
        args += [mod[0], mod[0]]
        specs += [mod_spec(mod[1][0]), mod_spec(mod[1][1])]
    out_shape, out_specs = [], []
    if write_x:
        out_shape.append(jax.ShapeDtypeStruct((t, d), F32))
        out_specs.append(row)
    out_shape.append(jax.ShapeDtypeStruct((t, d), out_dtype))
    out_specs.append(row)
    out = pl.pallas_call(
        functools.partial(_norm_kernel, has_res=res is not None, has_mod=mod is not None, write_x=write_x),
        grid=(t // bm,),
        in_specs=specs,
        out_specs=out_specs,
        out_shape=out_shape,
        compiler_params=_params("parallel"),
        name="norm_block",
    )(*args)
    return out if write_x else out[0]


def _mm_kernel(a_ref, b_ref, o_ref):
    o_ref[...] = jnp.dot(a_ref[...], b_ref[...], preferred_element_type=F32).astype(o_ref.dtype)


def matmul(a, b, *, out_dtype=BF16, bm=512, bn=1024):
    m, k = a.shape
    _, n = b.shape
    bm, bn = min(bm, m), min(bn, n)
    return pl.pallas_call(
        _mm_kernel,
        grid=(n // bn, m // bm),
        in_specs=[pl.BlockSpec((bm, k), lambda j, i: (i, 0)),
                  pl.BlockSpec((k, bn), lambda j, i: (0, j))],
        out_specs=pl.BlockSpec((bm, bn), lambda j, i: (i, j)),
        out_shape=jax.ShapeDtypeStruct((m, n), out_dtype),
        compiler_params=_params("parallel", "parallel"),
        name="matmul",
    )(a, b)


def _dft_mats(n):
    k = np.arange(n, dtype=np.int64)
    ang = 2.0 * np.pi * ((k[:, None] * k[None, :]) % n).astype(np.float64) / n
    s = 1.0 / math.sqrt(n)
    return (np.cos(ang) * s).astype(np.float32), (np.sin(ang) * s).astype(np.float32)


def _chan_dft_kernel(h_ref, w_ref, o_ref):
    gd = h_ref.shape[1]
    y = jnp.dot(h_ref[...], w_ref[...], preferred_element_type=F32)
    o_ref[0] = y[:, :gd].astype(o_ref.dtype)
    o_ref[1] = y[:, gd:].astype(o_ref.dtype)


def chan_dft(h, wcs, *, bm=512):
    t, d = h.shape
    gd = wcs.shape[0]
    bm = min(bm, t)
    return pl.pallas_call(
        _chan_dft_kernel,
        grid=(t // bm, d // gd),
        in_specs=[pl.BlockSpec((bm, gd), lambda i, g: (i, g)),
                  pl.BlockSpec((gd, 2 * gd), lambda i, g: (0, 0))],
        out_specs=pl.BlockSpec((2, bm, gd), lambda i, g: (0, i, g)),
        out_shape=jax.ShapeDtypeStruct((2, t, d), BF16),
        compiler_params=_params("parallel", "parallel"),
        name="chan_dft",
    )(h, wcs)


def _pos_dft_kernel(c_ref, s_ref, yc_ref, ys_ref, o_ref):
    acc = jnp.dot(c_ref[...], yc_ref[...], preferred_element_type=F32)
    acc = acc - jnp.dot(s_ref[...], ys_ref[...], preferred_element_type=F32)
    o_ref[...] = acc.astype(o_ref.dtype)


def pos_dft(cn, sn, y, n_seq, *, bm=512, bn=1024):
    n = cn.shape[0]
    _, t, d = y.shape
    bm, bn = min(bm, n), min(bn, d)
    nb = n // bm
    return pl.pallas_call(
        _pos_dft_kernel,
        grid=(n_seq, d // bn, nb),
        in_specs=[pl.BlockSpec((bm, n), lambda b, j, i: (i, 0)),
                  pl.BlockSpec((bm, n), lambda b, j, i: (i, 0)),
                  pl.BlockSpec((None, n, bn), lambda b, j, i: (0, b, j)),
                  pl.BlockSpec((None, n, bn), lambda b, j, i: (1, b, j))],
        out_specs=pl.BlockSpec((bm, bn), lambda b, j, i: (b * nb + i, j)),
        out_shape=jax.ShapeDtypeStruct((t, d), BF16),
        compiler_params=_params("parallel", "parallel", "parallel"),
        name="pos_dft",
    )(cn, sn, y, y)


def _rms_prologue(z, g):
    zn = z * lax.rsqrt(jnp.mean(z * z, axis=-1, keepdims=True) + EPS)
    return (zn * g).astype(BF16)


def _q_proj_kernel(z_ref, g_ref, w_ref, cs_ref, o_ref, zn_ref, *, scale):
    @pl.when(pl.program_id(1) == 0)
    def _():
        zn_ref[...] = _rms_prologue(z_ref[:, :Q_LORA], g_ref[...])

    acc = jnp.dot(zn_ref[...], w_ref[...], preferred_element_type=F32)
    cs = cs_ref[...]
    hw = QK_NOPE + 2 * QK_ROPE
    for h in range(w_ref.shape[1] // hw):
        o_ref[:, h * hw:h * hw + QK_NOPE] = (acc[:, h * hw:h * hw + QK_NOPE] * scale).astype(o_ref.dtype)
        t = acc[:, h * hw + QK_NOPE:(h + 1) * hw] * cs
        r = (t + pltpu.roll(t, QK_ROPE, axis=1)) * scale
        o_ref[:, h * hw + QK_NOPE:(h + 1) * hw] = r.astype(o_ref.dtype)


def q_proj(z, q_g, w_q, cs_tab, *, scale, bm=512, bn=1024):
    t, zw = z.shape
    n = w_q.shape[1]
    bm = min(bm, t)
    return pl.pallas_call(
        functools.partial(_q_proj_kernel, scale=scale),
        grid=(t // bm, n // bn),
        in_specs=[pl.BlockSpec((bm, zw), lambda i, j: (i, 0)),
                  pl.BlockSpec((1, Q_LORA), lambda i, j: (0, 0)),
                  pl.BlockSpec((Q_LORA, bn), lambda i, j: (0, j)),
                  pl.BlockSpec((bm, 2 * QK_ROPE), lambda i, j: (i, 0))],
        out_specs=pl.BlockSpec((bm, bn), lambda i, j: (i, j)),
        out_shape=jax.ShapeDtypeStruct((t, n), BF16),
        scratch_shapes=[pltpu.VMEM((bm, Q_LORA), BF16)],
        compiler_params=_params("parallel", "arbitrary"),
        name="q_proj",
    )(z, q_g.reshape(1, Q_LORA), w_q, cs_tab)


def _kv_proj_kernel(z_ref, g_ref, w_ref, cs_ref, o_ref, kpe_ref, zn_ref):
    @pl.when(pl.program_id(1) == 0)
    def _():
        zn_ref[...] = _rms_prologue(z_ref[:, Q_LORA:Q_LORA + KV_LORA], g_ref[...])
        t = z_ref[:, Q_LORA + KV_LORA:] * cs_ref[...]
        r = t + pltpu.roll(t, QK_ROPE, axis=1)
        lane = lax.broadcasted_iota(jnp.int32, r.shape, 1)
        kpe_ref[...] = jnp.where(lane < QK_ROPE, r, 0.0).astype(kpe_ref.dtype)

    o_ref[...] = jnp.dot(zn_ref[...], w_ref[...], preferred_element_type=F32).astype(o_ref.dtype)


def kv_proj(z, kv_g, w_kv, cs_tab, *, bm=512, bn=1024):
    t, zw = z.shape
    n = w_kv.shape[1]
    bm = min(bm, t)
    return pl.pallas_call(
        _kv_proj_kernel,
        grid=(t // bm, n // bn),
        in_specs=[pl.BlockSpec((bm, zw), lambda i, j: (i, 0)),
                  pl.BlockSpec((1, KV_LORA), lambda i, j: (0, 0)),
                  pl.BlockSpec((KV_LORA, bn), lambda i, j: (0, j)),
                  pl.BlockSpec((bm, 2 * QK_ROPE), lambda i, j: (i, 0))],
        out_specs=[pl.BlockSpec((bm, bn), lambda i, j: (i, j)),
                   pl.BlockSpec((bm, 2 * QK_ROPE), lambda i, j: (i, 0))],
        out_shape=[jax.ShapeDtypeStruct((t, n), BF16),
                   jax.ShapeDtypeStruct((t, 2 * QK_ROPE), BF16)],
        scratch_shapes=[pltpu.VMEM((bm, KV_LORA), BF16)],
        compiler_params=_params("parallel", "arbitrary"),
        name="kv_proj",
    )(z, kv_g.reshape(1, KV_LORA), w_kv, cs_tab)


def _attn_kernel(q_ref, kc_ref, pc_ref, vc_ref, kl_ref, pl_ref, vl_ref, o_ref, *, tq, tk):
    n = q_ref.shape[0]
    n_lat = kl_ref.shape[0]
    nt = (((1,), (1,)), ((), ()))

    def q_tile(i, carry):
        r0 = pl.multiple_of(i * tq, tq)
        q = q_ref[pl.ds(r0, tq), :]

        def chunk(k, v, m, l, acc):
            s = lax.dot_general(q, k, nt, preferred_element_type=F32)
            m_new = jnp.maximum(m, jnp.max(s, axis=-1, keepdims=True))
            alpha = jnp.exp2(m - m_new)
            p = jnp.exp2(s - m_new)
            l = alpha * l + jnp.sum(p, axis=-1, keepdims=True)
            acc = alpha * acc + jnp.dot(p.astype(BF16), v, preferred_element_type=F32)
            return m_new, l, acc

        m = jnp.full((tq, 1), NEG_INF, F32)
        l = jnp.zeros((tq, 1), F32)
        acc = jnp.zeros((tq, V_DIM), F32)
        kc = jnp.concatenate([kc_ref[...], pc_ref[...]], axis=1)
        m, l, acc = chunk(kc, vc_ref[...], m, l, acc)
        for c in range(n_lat // tk):
            k = jnp.concatenate([kl_ref[c * tk:(c + 1) * tk, :], pl_ref[c * tk:(c + 1) * tk, :]], axis=1)
            m, l, acc = chunk(k, vl_ref[c * tk:(c + 1) * tk, :], m, l, acc)
        o_ref[pl.ds(r0, tq), :] = (acc / l).astype(o_ref.dtype)
        return carry

    lax.fori_loop(0, n // tq, q_tile, 0)


def mla_attention(q, kv_c, kpe_c, kv_l, kpe_l, *, batch, heads, tq=512, tk=512):
    n = q.shape[0] // batch
    n_ctx = kv_c.shape[0] // batch
    tq, tk = min(tq, n), min(tk, n)
    hq = QK_NOPE + 2 * QK_ROPE
    return pl.pallas_call(
        functools.partial(_attn_kernel, tq=tq, tk=tk),
        grid=(batch, heads),
        in_specs=[pl.BlockSpec((n, hq), lambda b, h: (b, h)),
                  pl.BlockSpec((n_ctx, QK_NOPE), lambda b, h: (b, 2 * h)),
                  pl.BlockSpec((n_ctx, 2 * QK_ROPE), lambda b, h: (b, 0)),
                  pl.BlockSpec((n_ctx, V_DIM), lambda b, h: (b, 2 * h + 1)),
                  pl.BlockSpec((n, QK_NOPE), lambda b, h: (b, 2 * h)),
                  pl.BlockSpec((n, 2 * QK_ROPE), lambda b, h: (b, 0)),
                  pl.BlockSpec((n, V_DIM), lambda b, h: (b, 2 * h + 1))],
        out_specs=pl.BlockSpec((n, V_DIM), lambda b, h: (b, h)),
        out_shape=jax.ShapeDtypeStruct((batch * n, heads * V_DIM), BF16),
        compiler_params=_params("parallel", "parallel"),
        name="mla_attention",
    )(q, kv_c, kpe_c, kv_c, kv_l, kpe_l, kv_l)


def _peer_scores_kernel(h_ref, wq_ref, kk_ref, o_ref):
    q = jnp.dot(h_ref[...], wq_ref[...], preferred_element_type=F32)
    nt = (((1,), (1,)), ((), ()))
    for c in range(wq_ref.shape[1] // PEER_HALF):
        qc = q[:, c * PEER_HALF:(c + 1) * PEER_HALF]
        o_ref[c] = lax.dot_general(kk_ref[c], qc, nt, precision=lax.Precision.HIGHEST,
                                   preferred_element_type=F32)


def peer_scores(h, wq, kk, *, bm=512, bn=1024):
    t, d = h.shape
    n = wq.shape[1]
    bm, bn = min(bm, t), min(bn, n)
    cb = bn // PEER_HALF
    return pl.pallas_call(
        _peer_scores_kernel,
        grid=(n // bn, t // bm),
        in_specs=[pl.BlockSpec((bm, d), lambda j, i: (i, 0)),
                  pl.BlockSpec((d, bn), lambda j, i: (0, j)),
                  pl.BlockSpec((cb, N_KEYS, PEER_HALF), lambda j, i: (j, 0, 0))],
        out_specs=pl.BlockSpec((cb, N_KEYS, bm), lambda j, i: (j, 0, i)),
        out_shape=jax.ShapeDtypeStruct((n // PEER_HALF, N_KEYS, t), F32),
        compiler_params=_params("parallel", "parallel"),
        name="peer_scores",
    )(h, wq, kk)


_HALF_K = PEER_TOPK // 2


def _top16(s, iota):
    vals = []
    rank = jnp.full(s.shape, 255, jnp.int32)
    for r in range(PEER_TOPK):
        m = jnp.max(s, axis=0, keepdims=True)
        idx = jnp.min(jnp.where(s == m, iota, N_KEYS), axis=0, keepdims=True)
        hit = iota == idx
        rank = jnp.where(hit, r, rank)
        s = jnp.where(hit, NEG_INF, s)
        vals.append(m)
    return vals, rank


def _peer_topk_kernel(s_ref, k1_ref, k2_ref):
    tb = s_ref.shape[2]
    iota = lax.broadcasted_iota(jnp.int32, (N_KEYS, tb), 0)
    i16 = lax.broadcasted_iota(jnp.int32, (PEER_TOPK, tb), 0)
    i8 = lax.broadcasted_iota(jnp.int32, (_HALF_K, tb), 0)
    flat = jnp.concatenate([i16] + [i8 + PEER_TOPK * r for r in range(1, _HALF_K)]
                           + [(i8 + _HALF_K) * PEER_TOPK], axis=0)
    n_cand = flat.shape[0]
    for h in range(s_ref.shape[0] // 2):
        s1 = s_ref[2 * h]
        s2 = s_ref[2 * h + 1]
        a, rank1 = _top16(s1, iota)
        b, rank2 = _top16(s2, iota)
        a_hi = jnp.concatenate(a[_HALF_K:], axis=0)
        b_all = jnp.concatenate(b, axis=0)
        b_lo = b_all[:_HALF_K]
        cand = jnp.concatenate([a[0] + b_all] + [a[r] + b_lo for r in range(1, _HALF_K)]
                               + [a_hi + b[0]], axis=0)
        cmax = a[0] + b[0]
        sel = jnp.zeros((n_cand, tb), jnp.bool_)
        c = cand
        for _ in range(PEER_TOPK):
            m = jnp.max(c, axis=0, keepdims=True)
            idx = jnp.min(jnp.where(c == m, flat, PEER_TOPK * PEER_TOPK), axis=0, keepdims=True)
            hit = flat == idx
            sel = jnp.logical_or(sel, hit)
            c = jnp.where(hit, NEG_INF, c)
        self = sel.astype(F32)
        z = jnp.sum(self * jnp.exp(cand - cmax), axis=0, keepdims=True)
        counts = [jnp.sum(self[:PEER_TOPK], axis=0, keepdims=True)]
        for r in range(1, _HALF_K):
            lo = PEER_TOPK + (r - 1) * _HALF_K
            counts.append(jnp.sum(self[lo:lo + _HALF_K], axis=0, keepdims=True))
        tail = self[PEER_TOPK + (_HALF_K - 1) * _HALF_K:]
        counts += [tail[r:r + 1] for r in range(_HALF_K)]
        lim = jnp.zeros((N_KEYS, tb), F32)
        for r in range(PEER_TOPK):
            lim = jnp.where(rank1 == r, counts[r], lim)
        k1_ref[h, 0] = lim
        k1_ref[h, 1] = jnp.exp(s1 - a[0]) / z
        k2_ref[h, 0] = rank2.astype(F32)
        k2_ref[h, 1] = jnp.exp(s2 - b[0])


def peer_topk(scores, *, tb=256):
    c2, nk, t = scores.shape
    tb = min(tb, t)
    hh = c2 // 2
    return pl.pallas_call(
        _peer_topk_kernel,
        grid=(t // tb,),
        in_specs=[pl.BlockSpec((c2, nk, tb), lambda i: (0, 0, i))],
        out_specs=[pl.BlockSpec((hh, 2, nk, tb), lambda i: (0, 0, 0, i)),
                   pl.BlockSpec((hh, 2, nk, tb), lambda i: (0, 0, 0, i))],
        out_shape=[jax.ShapeDtypeStruct((hh, 2, nk, t), F32),
                   jax.ShapeDtypeStruct((hh, 2, nk, t), F32)],
        compiler_params=_params("parallel"),
        name="peer_topk",
    )(scores)


def _gelu(x):
    return 0.5 * x * (1.0 + lax.erf(x * (1.0 / math.sqrt(2.0))))


def _peer_dense_kernel(h_ref, u_ref, v_ref, k1_ref, k2_ref, o_ref):
    n1 = k1_ref.shape[2]
    tb = h_ref.shape[0]
    nt = (((1,), (1,)), ((), ()))
    tn = (((0,), (0,)), ((), ()))

    @pl.when(pl.program_id(1) == 0)
    def _():
        o_ref[...] = jnp.zeros_like(o_ref)

    a_t = lax.dot_general(u_ref[...], h_ref[...], nt, preferred_element_type=F32)
    parts = []
    for c in range(n1):
        g = jnp.zeros((N_KEYS, tb), F32)
        for h in range(k1_ref.shape[0]):
            count = k1_ref[h, 0, c:c + 1, :]
            p1 = k1_ref[h, 1, c:c + 1, :]
            g = g + jnp.where(k2_ref[h, 0] < count, k2_ref[h, 1] * p1, 0.0)
        parts.append(g)
    gate = jnp.concatenate(parts, axis=0) if n1 > 1 else parts[0]
    w_t = (gate * _gelu(a_t)).astype(BF16)
    o_ref[...] += lax.dot_general(w_t, v_ref[...], tn, preferred_element_type=F32)


def peer_dense(h, u, v, key1, key2, *, tb=512, ec=512):
    t, d = h.shape
    e = u.shape[0]
    tb, ec = min(tb, t), min(ec, e)
    hh, _, nk, _ = key1.shape
    n1 = ec // nk
    key1 = key1.reshape(hh, 2, nk // n1, n1, t)
    return pl.pallas_call(
        _peer_dense_kernel,
        grid=(t // tb, e // ec),
        in_specs=[pl.BlockSpec((tb, d), lambda i, j: (i, 0)),
                  pl.BlockSpec((ec, d), lambda i, j: (j, 0)),
                  pl.BlockSpec((ec, d), lambda i, j: (j, 0)),
                  pl.BlockSpec((hh, 2, None, n1, tb), lambda i, j: (0, 0, j, 0, i)),
                  pl.BlockSpec((hh, 2, nk, tb), lambda i, j: (0, 0, 0, i))],
        out_specs=pl.BlockSpec((tb, d), lambda i, j: (i, 0)),
        out_shape=jax.ShapeDtypeStruct((t, d), F32),
        compiler_params=_params("parallel", "arbitrary"),
        name="peer_dense",
    )(h, u, v, key1, key2)


def peer(h, wq, kk, u, v):
    scores = peer_scores(h, wq, kk)
    key1, key2 = peer_topk(scores)
    return peer_dense(h, u, v, key1, key2)


def _rope_tables(n):
    rows = n // GRID_W
    row_id = np.repeat(np.arange(rows), GRID_W).astype(np.float64)
    col_id = np.tile(np.arange(GRID_W), rows).astype(np.float64)
    axis_dim = QK_ROPE // 2
    inv_freq = ROPE_THETA ** (-np.arange(0, axis_dim, 2, dtype=np.float64) / axis_dim)
    ang = np.stack([row_id[:, None] * inv_freq, col_id[:, None] * inv_freq], axis=1)
    cos = np.cos(ang)
    sin = np.sin(ang)
    cos_full = np.concatenate([cos, cos], axis=-1).reshape(n, QK_ROPE)
    sin_full = np.concatenate([sin, sin], axis=-1).reshape(n, QK_ROPE)
    return np.concatenate([cos_full, sin_full], axis=-1).astype(np.float32)


def _rot_partner(w):
    ws = w.reshape(w.shape[:-1] + (2, 2, QK_ROPE // 4))
    return jnp.stack([-ws[..., 1, :], ws[..., 0, :]], axis=-2).reshape(w.shape)


def kernel(x, c, ctx, c_ctx, ada_w, ada_b, norm1_g, norm2_g, final_g, fnet_wo,
           mla_w_in, mla_q_g, mla_w_uq, mla_kv_g, mla_w_ukv, mla_wo,
           peer_wq, peer_k1, peer_k2, peer_u, peer_v):
    b, n, d = x.shape
    n_ctx = ctx.shape[1]
    depth = ada_w.shape[0]
    gd = d // F_GROUPS

    xl = x.reshape(b * n, d)
    xc = ctx.reshape(b * n_ctx, d)

    ct = jnp.zeros((d, SUBLANES), F32).at[:, :b].set(c.T).at[:, b].set(c_ctx)

    def sel_lat(i, bm):
        return (i * bm) // n

    def sel_ctx(i, bm):
        return b

    cc, sc = _dft_mats(gd)
    wcs = jnp.asarray(np.concatenate([cc, sc], axis=1)).astype(BF16)
    cn, sn = (jnp.asarray(m).astype(BF16) for m in _dft_mats(n))
    cn_c, sn_c = (jnp.asarray(m).astype(BF16) for m in _dft_mats(n_ctx))
    cs_lat = jnp.asarray(np.tile(_rope_tables(n), (b, 1)))
    cs_ctx = jnp.asarray(np.tile(np.concatenate([np.ones((1, QK_ROPE), np.float32),
                                                 np.zeros((1, QK_ROPE), np.float32)], axis=1),
                                 (b * n_ctx, 1)))
    q_scale = float((QK_NOPE + QK_ROPE) ** -0.5 * math.log2(math.e))

    pend_l = None
    pend_c = None
    for i in range(depth):
        last = i == depth - 1
        mod = ada_linear(ct, ada_w[i], ada_b[i], rows=b + 1)
        mod = mod.reshape(SUBLANES, ADA_CHUNKS, 1, d)

        if pend_l is None:
            hl = norm_block(xl, norm1_g[i], sel_lat, mod=(mod, (0, 1)))
            hc = norm_block(xc, norm1_g[i], sel_ctx, mod=(mod, (0, 1)))
        else:
            xl, hl = norm_block(xl, norm1_g[i], sel_lat, res=pend_l, mod=(mod, (0, 1)), write_x=True)
            xc, hc = norm_block(xc, norm1_g[i], sel_ctx, res=pend_c, mod=(mod, (0, 1)), write_x=True)
        j = i // 2
        if i % 2 == 0:
            wo = fnet_wo[j].astype(BF16)
            fl = pos_dft(cn, sn, chan_dft(hl, wcs), b)
            yl = matmul(fl, wo)
            if not last:
                fc = pos_dft(cn_c, sn_c, chan_dft(hc, wcs), b)
                yc = matmul(fc, wo)
        else:
            w_in = mla_w_in[j]
            w_in = jnp.concatenate([w_in, _rot_partner(w_in[:, Q_LORA + KV_LORA:])], axis=1).astype(BF16)
            wq3 = mla_w_uq[j].reshape(Q_LORA, MLA_HEADS, QK_NOPE + QK_ROPE)
            wq_ext = jnp.concatenate([wq3, _rot_partner(wq3[..., QK_NOPE:])], axis=-1)
            wq_ext = wq_ext.reshape(Q_LORA, MLA_HEADS * (QK_NOPE + 2 * QK_ROPE)).astype(BF16)
            w_kv = mla_w_ukv[j].astype(BF16)
            wo = mla_wo[j].astype(BF16)
            zl = matmul(hl, w_in, out_dtype=F32, bn=w_in.shape[1])
            zc = matmul(hc, w_in, out_dtype=F32, bn=w_in.shape[1])
            q = q_proj(zl, mla_q_g[j], wq_ext, cs_lat, scale=q_scale)
            kv_l, kpe_l = kv_proj(zl, mla_kv_g[j], w_kv, cs_lat)
            kv_c, kpe_c = kv_proj(zc, mla_kv_g[j], w_kv, cs_ctx)
            o = mla_attention(q, kv_c, kpe_c, kv_l, kpe_l, batch=b, heads=MLA_HEADS)
            yl = matmul(o, wo)
            if not last:
                raise NotImplementedError("context output of an MLA layer that is not last")
        wq = peer_wq[i].astype(BF16)
        kk = jnp.stack([peer_k1[i], peer_k2[i]], axis=1).reshape(2 * PEER_HEADS, N_KEYS, PEER_HALF)
        u = peer_u[i].astype(BF16)
        v = peer_v[i].astype(BF16)
        xl, h2l = norm_block(xl, norm2_g[i], sel_lat, res=(yl, mod, 2), mod=(mod, (3, 4)), write_x=True)
        pend_l = (peer(h2l, wq, kk, u, v), mod, 5)
        if not last:
            xc, h2c = norm_block(xc, norm2_g[i], sel_ctx, res=(yc, mod, 2), mod=(mod, (3, 4)), write_x=True)
            pend_c = (peer(h2c, wq, kk, u, v), mod, 5)

    out = norm_block(xl, final_g, sel_lat, res=pend_l, out_dtype=F32)
    return out.reshape(b, n, d)
```

```python
import functools
import math

import numpy as np
import jax
import jax.numpy as jnp
from jax import lax
from jax.experimental import pallas as pl
from jax.experimental.pallas import tpu as pltpu

F32 = jnp.float32
BF16 = jnp.bfloat16

EPS = 1e-6
ADA_CHUNKS = 6
F_GROUPS = 8
GRID_W = 64
MLA_HEADS = 32
QK_NOPE = 128
QK_ROPE = 64
V_DIM = 128
Q_LORA = 1024
KV_LORA = 512
ROPE_THETA = 10000.0
PEER_HEADS = 8
PEER_HALF = 128
N_KEYS = 128
PEER_TOPK = 16

LANES = 128
SUBLANES = 8
VMEM_LIMIT = 56 * 1024 * 1024
NEG_INF = float("-inf")


def _params(*sem):
    return pltpu.CompilerParams(dimension_semantics=sem, vmem_limit_bytes=VMEM_LIMIT)


def _ada_kernel(ct_ref, w_ref, b_ref, o_ref, sb_ref, *, kc, rows):
    d, bn = w_ref.shape
    nt = bn // LANES

    @pl.when(pl.program_id(0) == 0)
    def _():
        c = ct_ref[...]
        s = c * jax.nn.sigmoid(c)
        for r in range(rows):
            sb_ref[r] = jnp.broadcast_to(s[:, r:r + 1], (d, LANES))

    def body(i, accs):
        k0 = pl.multiple_of(i * kc, kc)
        out = list(accs)
        for t in range(nt):
            wv = w_ref[pl.ds(k0, kc), t * LANES:(t + 1) * LANES]
            for r in range(rows):
                p = wv * sb_ref[r, pl.ds(k0, kc), :]
                out[r * nt + t] = out[r * nt + t] + p.reshape(kc // SUBLANES, SUBLANES, LANES).sum(axis=0)
        return tuple(out)

    init = tuple(jnp.zeros((SUBLANES, LANES), F32) for _ in range(rows * nt))
    accs = lax.fori_loop(0, d // kc, body, init)
    o_ref[...] = jnp.zeros_like(o_ref)
    for r in range(rows):
        for t in range(nt):
            row = accs[r * nt + t].sum(axis=0, keepdims=True) + b_ref[:, t * LANES:(t + 1) * LANES]
            o_ref[r:r + 1, t * LANES:(t + 1) * LANES] = row


def ada_linear(ct, w, b, layer, *, rows, bn=512, kc=256):
    nl, d, n = w.shape
    bn = min(bn, n)
    kc = min(kc, d)
    return pl.pallas_call(
        functools.partial(_ada_kernel, kc=kc, rows=rows),
        grid=(n // bn,),
        in_specs=[
            pl.BlockSpec((d, SUBLANES), lambda j: (0, 0)),
            pl.BlockSpec((None, d, bn), lambda j: (layer, 0, j)),
            pl.BlockSpec((None, 1, bn), lambda j: (layer, 0, j)),
        ],
        out_specs=pl.BlockSpec((SUBLANES, bn), lambda j: (0, j)),
        out_shape=jax.ShapeDtypeStruct((SUBLANES, n), F32),
        scratch_shapes=[pltpu.VMEM((rows, d, LANES), F32)],
        compiler_params=_params("arbitrary"),
        name="ada_linear",
    )(ct, w, b.reshape(nl, 1, n))


def _cast_kernel(w_ref, o_ref):
    o_ref[...] = w_ref[...].astype(o_ref.dtype)


def cast_layer(w, layer, *, bm=512):
    _, r, c = w.shape
    bm = min(bm, r)
    return pl.pallas_call(
        _cast_kernel,
        grid=(r // bm,),
        in_specs=[pl.BlockSpec((None, bm, c), lambda i: (layer, i, 0))],
        out_specs=pl.BlockSpec((bm, c), lambda i: (i, 0)),
        out_shape=jax.ShapeDtypeStruct((r, c), BF16),
        compiler_params=_params("parallel"),
        name="cast_layer",
    )(w)


def _norm_kernel(*refs, has_res, has_mod, write_x):
    it = iter(refs)
    x_ref = next(it)
    if has_res:
        y_ref, gate_ref = next(it), next(it)
    g_ref = next(it)
    if has_mod:
        shift_ref, scale_ref = next(it), next(it)
    if write_x:
        xo_ref = next(it)
    h_ref = next(it)

    x = x_ref[...]
    if has_res:
        x = x + gate_ref[...] * y_ref[...].astype(F32)
    if write_x:
        xo_ref[...] = x
    y = x * lax.rsqrt(jnp.mean(x * x, axis=-1, keepdims=True) + EPS)
    y = y * g_ref[...]
    if has_mod:
        y = y * (1.0 + scale_ref[...]) + shift_ref[...]
    h_ref[...] = y.astype(h_ref.dtype)


def norm_block(x, g, sel, *, res=None, mod=None, write_x=False, out_dtype=BF16, bm=256):
    t, d = x.shape
    bm = min(bm, t)
    row = pl.BlockSpec((bm, d), lambda i: (i, 0))

    def mod_spec(a):
        return pl.BlockSpec((None, None, 1, d), lambda i, a=a: (sel(i, bm), a, 0, 0))

    args, specs = [x], [row]
    if res is not None:
        args += [res[0], res[1]]
        specs += [row, mod_spec(res[2])]
    args.append(g.reshape(1, d))
    specs.append(pl.BlockSpec((1, d), lambda i: (0, 0)))
    if mod is not None:
        args += [mod[0], mod[0]]
        specs += [mod_spec(mod[1][0]), mod_spec(mod[1][1])]
    out_shape, out_specs = [], []
    if write_x:
        out_shape.append(jax.ShapeDtypeStruct((t, d), F32))
        out_specs.append(row)
    out_shape.append(jax.ShapeDtypeStruct((t, d), out_dtype))
    out_specs.append(row)
    out = pl.pallas_call(
        functools.partial(_norm_kernel, has_res=res is not None, has_mod=mod is not None, write_x=write_x),
        grid=(t // bm,),
        in_specs=specs,
        out_specs=out_specs,
        out_shape=out_shape,
        compiler_params=_params("parallel"),
        name="norm_block",
    )(*args)
    return out if write_x else out[0]


def _mm_kernel(a_ref, b_ref, o_ref):
    o_ref[...] = jnp.dot(a_ref[...], b_ref[...], preferred_element_type=F32).astype(o_ref.dtype)


def matmul(a, b, *, out_dtype=BF16, bm=512, bn=1024):
    m, k = a.shape
    _, n = b.shape
    bm, bn = min(bm, m), min(bn, n)
    return pl.pallas_call(
        _mm_kernel,
        grid=(n // bn, m // bm),
        in_specs=[pl.BlockSpec((bm, k), lambda j, i: (i, 0)),
                  pl.BlockSpec((k, bn), lambda j, i: (0, j))],
        out_specs=pl.BlockSpec((bm, bn), lambda j, i: (i, j)),
        out_shape=jax.ShapeDtypeStruct((m, n), out_dtype),
        compiler_params=_params("parallel", "parallel"),
        name="matmul",
    )(a, b)


def _dft_mats(n):
    k = np.arange(n, dtype=np.int64)
    ang = 2.0 * np.pi * ((k[:, None] * k[None, :]) % n).astype(np.float64) / n
    s = 1.0 / math.sqrt(n)
    return (np.cos(ang) * s).astype(np.float32), (np.sin(ang) * s).astype(np.float32)


def _chan_dft_kernel(h_ref, w_ref, o_ref):
    gd = h_ref.shape[1]
    y = jnp.dot(h_ref[...], w_ref[...], preferred_element_type=F32)
    o_ref[0] = y[:, :gd].astype(o_ref.dtype)
    o_ref[1] = y[:, gd:].astype(o_ref.dtype)


def chan_dft(h, wcs, *, bm=512):
    t, d = h.shape
    gd = wcs.shape[0]
    bm = min(bm, t)
    return pl.pallas_call(
        _chan_dft_kernel,
        grid=(t // bm, d // gd),
        in_specs=[pl.BlockSpec((bm, gd), lambda i, g: (i, g)),
                  pl.BlockSpec((gd, 2 * gd), lambda i, g: (0, 0))],
        out_specs=pl.BlockSpec((2, bm, gd), lambda i, g: (0, i, g)),
        out_shape=jax.ShapeDtypeStruct((2, t, d), BF16),
        compiler_params=_params("parallel", "parallel"),
        name="chan_dft",
    )(h, wcs)


def _pos_dft_mats(n):
    cn, sn = _dft_mats(n)
    half = n // 2
    me = np.concatenate([cn[:half, 0::2], -sn[:half, 0::2]], axis=1)
    mo = np.concatenate([cn[:half, 1::2], -sn[:half, 1::2]], axis=1)
    return me, mo


def _pos_dft_kernel(me_ref, mo_ref, yce_ref, yse_ref, yco_ref, yso_ref, o_ref):
    ye = jnp.concatenate([yce_ref[...], yse_ref[...]], axis=0)
    yo = jnp.concatenate([yco_ref[...], yso_ref[...]], axis=0)
    e = jnp.dot(me_ref[...], ye, preferred_element_type=F32)
    o = jnp.dot(mo_ref[...], yo, preferred_element_type=F32)
    o_ref[0] = (e + o).astype(o_ref.dtype)
    o_ref[1] = (e - o).astype(o_ref.dtype)


def pos_dft(me, mo, y, n_seq, *, bm=512, bn=512):
    half, n = me.shape
    _, t, d = y.shape
    bm, bn = min(bm, half), min(bn, d)
    nb = d // bn
    y2 = y.reshape(2, t // 2, 2 * d)
    out = pl.pallas_call(
        _pos_dft_kernel,
        grid=(n_seq, nb, half // bm),
        in_specs=[pl.BlockSpec((bm, n), lambda b, j, i: (i, 0)),
                  pl.BlockSpec((bm, n), lambda b, j, i: (i, 0)),
                  pl.BlockSpec((None, half, bn), lambda b, j, i: (0, b, j)),
                  pl.BlockSpec((None, half, bn), lambda b, j, i: (1, b, j)),
                  pl.BlockSpec((None, half, bn), lambda b, j, i: (0, b, nb + j)),
                  pl.BlockSpec((None, half, bn), lambda b, j, i: (1, b, nb + j))],
        out_specs=pl.BlockSpec((None, 2, bm, bn), lambda b, j, i: (b, 0, i, j)),
        out_shape=jax.ShapeDtypeStruct((n_seq, 2, half, d), BF16),
        compiler_params=_params("parallel", "parallel", "parallel"),
        name="pos_dft",
    )(me, mo, y2, y2, y2, y2)
    return out.reshape(t, d)


def _rms_prologue(z, g):
    zn = z * lax.rsqrt(jnp.mean(z * z, axis=-1, keepdims=True) + EPS)
    return (zn * g).astype(BF16)


def _q_proj_kernel(z_ref, g_ref, w_ref, cs_ref, o_ref, zn_ref, *, scale):
    @pl.when(pl.program_id(1) == 0)
    def _():
        zn_ref[...] = _rms_prologue(z_ref[:, :Q_LORA], g_ref[...])

    acc = jnp.dot(zn_ref[...], w_ref[...], preferred_element_type=F32)
    cs = cs_ref[...]
    hw = QK_NOPE + 2 * QK_ROPE
    for h in range(w_ref.shape[1] // hw):
        o_ref[:, h * hw:h * hw + QK_NOPE] = (acc[:, h * hw:h * hw + QK_NOPE] * scale).astype(o_ref.dtype)
        t = acc[:, h * hw + QK_NOPE:(h + 1) * hw] * cs
        r = (t + pltpu.roll(t, QK_ROPE, axis=1)) * scale
        o_ref[:, h * hw + QK_NOPE:(h + 1) * hw] = r.astype(o_ref.dtype)


def q_proj(z, q_g, w_q, cs_tab, *, scale, bm=512, bn=1024):
    t, zw = z.shape
    n = w_q.shape[1]
    bm = min(bm, t)
    return pl.pallas_call(
        functools.partial(_q_proj_kernel, scale=scale),
        grid=(t // bm, n // bn),
        in_specs=[pl.BlockSpec((bm, zw), lambda i, j: (i, 0)),
                  pl.BlockSpec((1, Q_LORA), lambda i, j: (0, 0)),
                  pl.BlockSpec((Q_LORA, bn), lambda i, j: (0, j)),
                  pl.BlockSpec((bm, 2 * QK_ROPE), lambda i, j: (i, 0))],
        out_specs=pl.BlockSpec((bm, bn), lambda i, j: (i, j)),
        out_shape=jax.ShapeDtypeStruct((t, n), BF16),
        scratch_shapes=[pltpu.VMEM((bm, Q_LORA), BF16)],
        compiler_params=_params("parallel", "arbitrary"),
        name="q_proj",
    )(z, q_g.reshape(1, Q_LORA), w_q, cs_tab)


def _kv_proj_kernel(z_ref, g_ref, w_ref, cs_ref, o_ref, kpe_ref, zn_ref):
    @pl.when(pl.program_id(1) == 0)
    def _():
        zn_ref[...] = _rms_prologue(z_ref[:, Q_LORA:Q_LORA + KV_LORA], g_ref[...])
        t = z_ref[:, Q_LORA + KV_LORA:] * cs_ref[...]
        r = t + pltpu.roll(t, QK_ROPE, axis=1)
        lane = lax.broadcasted_iota(jnp.int32, r.shape, 1)
        kpe_ref[...] = jnp.where(lane < QK_ROPE, r, 0.0).astype(kpe_ref.dtype)

    o_ref[...] = jnp.dot(zn_ref[...], w_ref[...], preferred_element_type=F32).astype(o_ref.dtype)


def kv_proj(z, kv_g, w_kv, cs_tab, *, bm=512, bn=1024):
    t, zw = z.shape
    n = w_kv.shape[1]
    bm = min(bm, t)
    return pl.pallas_call(
        _kv_proj_kernel,
        grid=(t // bm, n // bn),
        in_specs=[pl.BlockSpec((bm, zw), lambda i, j: (i, 0)),
                  pl.BlockSpec((1, KV_LORA), lambda i, j: (0, 0)),
                  pl.BlockSpec((KV_LORA, bn), lambda i, j: (0, j)),
                  pl.BlockSpec((bm, 2 * QK_ROPE), lambda i, j: (i, 0))],
        out_specs=[pl.BlockSpec((bm, bn), lambda i, j: (i, j)),
                   pl.BlockSpec((bm, 2 * QK_ROPE), lambda i, j: (i, 0))],
        out_shape=[jax.ShapeDtypeStruct((t, n), BF16),
                   jax.ShapeDtypeStruct((t, 2 * QK_ROPE), BF16)],
        scratch_shapes=[pltpu.VMEM((bm, KV_LORA), BF16)],
        compiler_params=_params("parallel", "arbitrary"),
        name="kv_proj",
    )(z, kv_g.reshape(1, KV_LORA), w_kv, cs_tab)


def _attn_kernel(q_ref, kc_ref, pc_ref, vc_ref, kl_ref, pl_ref, vl_ref, o_ref, *, tq, tk):
    n = q_ref.shape[0]
    n_lat = kl_ref.shape[0]
    nt = (((1,), (1,)), ((), ()))

    def q_tile(i, carry):
        r0 = pl.multiple_of(i * tq, tq)
        q = q_ref[pl.ds(r0, tq), :]

        def chunk(k, v, m, l, acc):
            s = lax.dot_general(q, k, nt, preferred_element_type=F32)
            m_new = jnp.maximum(m, jnp.max(s, axis=-1, keepdims=True))
            alpha = jnp.exp2(m - m_new)
            p = jnp.exp2(s - m_new)
            l = alpha * l + jnp.sum(p, axis=-1, keepdims=True)
            acc = alpha * acc + jnp.dot(p.astype(BF16), v, preferred_element_type=F32)
            return m_new, l, acc

        m = jnp.full((tq, 1), NEG_INF, F32)
        l = jnp.zeros((tq, 1), F32)
        acc = jnp.zeros((tq, V_DIM), F32)
        kc = jnp.concatenate([kc_ref[...], pc_ref[...]], axis=1)
        m, l, acc = chunk(kc, vc_ref[...], m, l, acc)
        for c in range(n_lat // tk):
            k = jnp.concatenate([kl_ref[c * tk:(c + 1) * tk, :], pl_ref[c * tk:(c + 1) * tk, :]], axis=1)
            m, l, acc = chunk(k, vl_ref[c * tk:(c + 1) * tk, :], m, l, acc)
        o_ref[pl.ds(r0, tq), :] = (acc / l).astype(o_ref.dtype)
        return carry

    lax.fori_loop(0, n // tq, q_tile, 0)


def mla_attention(q, kv_c, kpe_c, kv_l, kpe_l, *, batch, heads, tq=512, tk=512):
    n = q.shape[0] // batch
    n_ctx = kv_c.shape[0] // batch
    tq, tk = min(tq, n), min(tk, n)
    hq = QK_NOPE + 2 * QK_ROPE
    return pl.pallas_call(
        functools.partial(_attn_kernel, tq=tq, tk=tk),
        grid=(batch, heads),
        in_specs=[pl.BlockSpec((n, hq), lambda b, h: (b, h)),
                  pl.BlockSpec((n_ctx, QK_NOPE), lambda b, h: (b, 2 * h)),
                  pl.BlockSpec((n_ctx, 2 * QK_ROPE), lambda b, h: (b, 0)),
                  pl.BlockSpec((n_ctx, V_DIM), lambda b, h: (b, 2 * h + 1)),
                  pl.BlockSpec((n, QK_NOPE), lambda b, h: (b, 2 * h)),
                  pl.BlockSpec((n, 2 * QK_ROPE), lambda b, h: (b, 0)),
                  pl.BlockSpec((n, V_DIM), lambda b, h: (b, 2 * h + 1))],
        out_specs=pl.BlockSpec((n, V_DIM), lambda b, h: (b, h)),
        out_shape=jax.ShapeDtypeStruct((batch * n, heads * V_DIM), BF16),
        compiler_params=_params("parallel", "parallel"),
        name="mla_attention",
    )(q, kv_c, kpe_c, kv_c, kv_l, kpe_l, kv_l)


def _peer_scores_kernel(h_ref, wq_ref, kk_ref, o_ref):
    q = jnp.dot(h_ref[...], wq_ref[...], preferred_element_type=F32)
    nt = (((1,), (1,)), ((), ()))
    for c in range(wq_ref.shape[1] // PEER_HALF):
        qc = q[:, c * PEER_HALF:(c + 1) * PEER_HALF]
        o_ref[c] = lax.dot_general(kk_ref[c], qc, nt, precision=lax.Precision.HIGHEST,
                                   preferred_element_type=F32)


def peer_scores(h, wq, kk, *, bm=512, bn=1024):
    t, d = h.shape
    n = wq.shape[1]
    bm, bn = min(bm, t), min(bn, n)
    cb = bn // PEER_HALF
    return pl.pallas_call(
        _peer_scores_kernel,
        grid=(n // bn, t // bm),
        in_specs=[pl.BlockSpec((bm, d), lambda j, i: (i, 0)),
                  pl.BlockSpec((d, bn), lambda j, i: (0, j)),
                  pl.BlockSpec((cb, N_KEYS, PEER_HALF), lambda j, i: (j, 0, 0))],
        out_specs=pl.BlockSpec((cb, N_KEYS, bm), lambda j, i: (j, 0, i)),
        out_shape=jax.ShapeDtypeStruct((n // PEER_HALF, N_KEYS, t), F32),
        compiler_params=_params("parallel", "parallel"),
        name="peer_scores",
    )(h, wq, kk)


_HALF_K = PEER_TOPK // 2


def _top16(s, iota):
    vals = []
    rank = jnp.full(s.shape, 255, jnp.int32)
    for r in range(PEER_TOPK):
        m = jnp.max(s, axis=0, keepdims=True)
        idx = jnp.min(jnp.where(s == m, iota, N_KEYS), axis=0, keepdims=True)
        hit = iota == idx
        rank = jnp.where(hit, r, rank)
        s = jnp.where(hit, NEG_INF, s)
        vals.append(m)
    return vals, rank


def _extract16(s):
    vals = []
    for _ in range(PEER_TOPK):
        m = jnp.max(s, axis=0, keepdims=True)
        vals.append(m)
        s = jnp.where(s == m, NEG_INF, s)
    removed = jnp.sum(jnp.where(s == NEG_INF, 1.0, 0.0), axis=0, keepdims=True)
    return vals, removed != float(PEER_TOPK)


def _candidates(a, b):
    a_hi = jnp.concatenate(a[_HALF_K:], axis=0)
    b_all = jnp.concatenate(b, axis=0)
    b_lo = b_all[:_HALF_K]
    cand = jnp.concatenate([a[0] + b_all] + [a[r] + b_lo for r in range(1, _HALF_K)]
                           + [a_hi + b[0]], axis=0)
    return cand, b_all, b_lo


def _peer_topk_fast(s_ref, k1_ref, k2_ref):
    tb = s_ref.shape[2]
    bad = jnp.zeros((1, tb), jnp.bool_)
    for h in range(s_ref.shape[0] // 2):
        s1 = s_ref[2 * h]
        s2 = s_ref[2 * h + 1]
        a, bad1 = _extract16(s1)
        b, bad2 = _extract16(s2)
        cand, b_all, b_lo = _candidates(a, b)
        sums, bad3 = _extract16(cand)
        bad = bad | bad1 | bad2 | bad3
        c16 = sums[-1]
        keep = cand >= c16
        z = jnp.sum(jnp.where(keep, jnp.exp(cand - sums[0]), 0.0), axis=0, keepdims=True)
        inf = float("inf")
        thr = [jnp.min(jnp.where(keep[:PEER_TOPK], b_all, inf), axis=0, keepdims=True)]
        for r in range(1, _HALF_K):
            lo = PEER_TOPK + (r - 1) * _HALF_K
            thr.append(jnp.min(jnp.where(keep[lo:lo + _HALF_K], b_lo, inf), axis=0, keepdims=True))
        lo = PEER_TOPK + (_HALF_K - 1) * _HALF_K
        thr += [jnp.where(keep[lo + r:lo + r + 1], b[0], inf) for r in range(_HALF_K)]
        th = jnp.full((N_KEYS, tb), inf, F32)
        for r in range(PEER_TOPK):
            th = jnp.where(s1 == a[r], thr[r], th)
        k1_ref[h, 0] = th
        k1_ref[h, 1] = jnp.exp(s1 - a[0]) * (1.0 / z)
        k2_ref[h, 0] = s2
        k2_ref[h, 1] = jnp.exp(s2 - b[0])
    return bad


def _peer_topk_kernel(s_ref, k1_ref, k2_ref):
    bad = _peer_topk_fast(s_ref, k1_ref, k2_ref)

    @pl.when(jnp.max(jnp.where(bad, 1.0, 0.0)) > 0.0)
    def _():
        _peer_topk_exact(s_ref, k1_ref, k2_ref)


def _peer_topk_exact(s_ref, k1_ref, k2_ref):
    tb = s_ref.shape[2]
    iota = lax.broadcasted_iota(jnp.int32, (N_KEYS, tb), 0)
    i16 = lax.broadcasted_iota(jnp.int32, (PEER_TOPK, tb), 0)
    i8 = lax.broadcasted_iota(jnp.int32, (_HALF_K, tb), 0)
    flat = jnp.concatenate([i16] + [i8 + PEER_TOPK * r for r in range(1, _HALF_K)]
                           + [(i8 + _HALF_K) * PEER_TOPK], axis=0)
    n_cand = flat.shape[0]
    for h in range(s_ref.shape[0] // 2):
        s1 = s_ref[2 * h]
        s2 = s_ref[2 * h + 1]
        a, rank1 = _top16(s1, iota)
        b, rank2 = _top16(s2, iota)
        cand, _, _ = _candidates(a, b)
        cmax = a[0] + b[0]
        sel = jnp.zeros((n_cand, tb), jnp.bool_)
        c = cand
        for _ in range(PEER_TOPK):
            m = jnp.max(c, axis=0, keepdims=True)
            idx = jnp.min(jnp.where(c == m, flat, PEER_TOPK * PEER_TOPK), axis=0, keepdims=True)
            hit = flat == idx
            sel = jnp.logical_or(sel, hit)
            c = jnp.where(hit, NEG_INF, c)
        self = sel.astype(F32)
        z = jnp.sum(self * jnp.exp(cand - cmax), axis=0, keepdims=True)
        counts = [jnp.sum(self[:PEER_TOPK], axis=0, keepdims=True)]
        for r in range(1, _HALF_K):
            lo = PEER_TOPK + (r - 1) * _HALF_K
            counts.append(jnp.sum(self[lo:lo + _HALF_K], axis=0, keepdims=True))
        tail = self[PEER_TOPK + (_HALF_K - 1) * _HALF_K:]
        counts += [tail[r:r + 1] for r in range(_HALF_K)]
        lim = jnp.zeros((N_KEYS, tb), F32)
        for r in range(PEER_TOPK):
            lim = jnp.where(rank1 == r, counts[r], lim)
        k1_ref[h, 0] = 1.0 - lim
        k1_ref[h, 1] = jnp.exp(s1 - a[0]) * (1.0 / z)
        k2_ref[h, 0] = -rank2.astype(F32)
        k2_ref[h, 1] = jnp.exp(s2 - b[0])


def peer_topk(scores, *, tb=256):
    c2, nk, t = scores.shape
    tb = min(tb, t)
    hh = c2 // 2
    return pl.pallas_call(
        _peer_topk_kernel,
        grid=(t // tb,),
        in_specs=[pl.BlockSpec((c2, nk, tb), lambda i: (0, 0, i))],
        out_specs=[pl.BlockSpec((hh, 2, nk, tb), lambda i: (0, 0, 0, i)),
                   pl.BlockSpec((hh, 2, nk, tb), lambda i: (0, 0, 0, i))],
        out_shape=[jax.ShapeDtypeStruct((hh, 2, nk, t), F32),
                   jax.ShapeDtypeStruct((hh, 2, nk, t), F32)],
        compiler_params=_params("parallel"),
        name="peer_topk",
    )(scores)


def _gelu(x):
    return 0.5 * x * (1.0 + lax.erf(x * (1.0 / math.sqrt(2.0))))


def _peer_dense_kernel(h_ref, u_ref, v_ref, k1_ref, k2_ref, o_ref):
    j = pl.program_id(1)
    n1 = u_ref.shape[0] // N_KEYS
    tb = h_ref.shape[0]
    nt = (((1,), (1,)), ((), ()))
    tn = (((0,), (0,)), ((), ()))
    odd = (j % 2) == 1

    @pl.when(j == 0)
    def _():
        o_ref[...] = jnp.zeros_like(o_ref)

    a_t = lax.dot_general(u_ref[...], h_ref[...], nt, preferred_element_type=F32)
    parts = []
    for c in range(n1):
        g = jnp.zeros((N_KEYS, tb), F32)
        for h in range(k1_ref.shape[0]):
            lo, hi = k1_ref[h, :, c:c + 1, :], k1_ref[h, :, n1 + c:n1 + c + 1, :]
            thr = jnp.where(odd, hi[0], lo[0])
            p1 = jnp.where(odd, hi[1], lo[1])
            g = g + jnp.where(k2_ref[h, 0] >= thr, k2_ref[h, 1] * p1, 0.0)
        parts.append(g)
    gate = jnp.concatenate(parts, axis=0)
    w_t = (gate * _gelu(a_t)).astype(BF16)
    o_ref[...] += lax.dot_general(w_t, v_ref[...], tn, preferred_element_type=F32)


def peer_dense(h, u, v, key1, key2, *, tb=512, ec=512):
    t, d = h.shape
    e = u.shape[0]
    tb = min(tb, t)
    hh, _, nk, _ = key1.shape
    assert 2 * (ec // nk) == SUBLANES
    return pl.pallas_call(
        _peer_dense_kernel,
        grid=(t // tb, e // ec),
        in_specs=[pl.BlockSpec((tb, d), lambda i, j: (i, 0)),
                  pl.BlockSpec((ec, d), lambda i, j: (j, 0)),
                  pl.BlockSpec((ec, d), lambda i, j: (j, 0)),
                  pl.BlockSpec((hh, 2, SUBLANES, tb), lambda i, j: (0, 0, j // 2, i)),
                  pl.BlockSpec((hh, 2, nk, tb), lambda i, j: (0, 0, 0, i))],
        out_specs=pl.BlockSpec((tb, d), lambda i, j: (i, 0)),
        out_shape=jax.ShapeDtypeStruct((t, d), F32),
        compiler_params=_params("parallel", "arbitrary"),
        name="peer_dense",
    )(h, u, v, key1, key2)


def peer(h, wq, kk, u, v):
    scores = peer_scores(h, wq, kk)
    key1, key2 = peer_topk(scores)
    return peer_dense(h, u, v, key1, key2)


def _rope_tables(n):
    rows = n // GRID_W
    row_id = np.repeat(np.arange(rows), GRID_W).astype(np.float64)
    col_id = np.tile(np.arange(GRID_W), rows).astype(np.float64)
    axis_dim = QK_ROPE // 2
    inv_freq = ROPE_THETA ** (-np.arange(0, axis_dim, 2, dtype=np.float64) / axis_dim)
    ang = np.stack([row_id[:, None] * inv_freq, col_id[:, None] * inv_freq], axis=1)
    cos = np.cos(ang)
    sin = np.sin(ang)
    cos_full = np.concatenate([cos, cos], axis=-1).reshape(n, QK_ROPE)
    sin_full = np.concatenate([sin, sin], axis=-1).reshape(n, QK_ROPE)
    return np.concatenate([cos_full, sin_full], axis=-1).astype(np.float32)


def _rot_partner(w):
    ws = w.reshape(w.shape[:-1] + (2, 2, QK_ROPE // 4))
    return jnp.stack([-ws[..., 1, :], ws[..., 0, :]], axis=-2).reshape(w.shape)


def kernel(x, c, ctx, c_ctx, ada_w, ada_b, norm1_g, norm2_g, final_g, fnet_wo,
           mla_w_in, mla_q_g, mla_w_uq, mla_kv_g, mla_w_ukv, mla_wo,
           peer_wq, peer_k1, peer_k2, peer_u, peer_v):
    b, n, d = x.shape
    n_ctx = ctx.shape[1]
    depth = ada_w.shape[0]
    gd = d // F_GROUPS

    xl = x.reshape(b * n, d)
    xc = ctx.reshape(b * n_ctx, d)

    ct = jnp.zeros((d, SUBLANES), F32).at[:, :b].set(c.T).at[:, b].set(c_ctx)

    def sel_lat(i, bm):
        return (i * bm) // n

    def sel_ctx(i, bm):
        return b

    cc, sc = _dft_mats(gd)
    wcs = jnp.asarray(np.concatenate([cc, sc], axis=1)).astype(BF16)
    me_l, mo_l = (jnp.asarray(m).astype(BF16) for m in _pos_dft_mats(n))
    me_c, mo_c = (jnp.asarray(m).astype(BF16) for m in _pos_dft_mats(n_ctx))
    cs_lat = jnp.asarray(np.tile(_rope_tables(n), (b, 1)))
    cs_ctx = jnp.asarray(np.tile(np.concatenate([np.ones((1, QK_ROPE), np.float32),
                                                 np.zeros((1, QK_ROPE), np.float32)], axis=1),
                                 (b * n_ctx, 1)))
    q_scale = float((QK_NOPE + QK_ROPE) ** -0.5 * math.log2(math.e))

    pend_l = None
    pend_c = None
    for i in range(depth):
        last = i == depth - 1
        mod = ada_linear(ct, ada_w, ada_b, i, rows=b + 1)
        mod = mod.reshape(SUBLANES, ADA_CHUNKS, 1, d)

        if pend_l is None:
            hl = norm_block(xl, norm1_g[i], sel_lat, mod=(mod, (0, 1)))
            hc = norm_block(xc, norm1_g[i], sel_ctx, mod=(mod, (0, 1)))
        else:
            xl, hl = norm_block(xl, norm1_g[i], sel_lat, res=pend_l, mod=(mod, (0, 1)), write_x=True)
            xc, hc = norm_block(xc, norm1_g[i], sel_ctx, res=pend_c, mod=(mod, (0, 1)), write_x=True)
        j = i // 2
        if i % 2 == 0:
            wo = cast_layer(fnet_wo, j)
            fl = pos_dft(me_l, mo_l, chan_dft(hl, wcs), b)
            yl = matmul(fl, wo)
            if not last:
                fc = pos_dft(me_c, mo_c, chan_dft(hc, wcs), b)
                yc = matmul(fc, wo)
        else:
            w_in = mla_w_in[j]
            w_in = jnp.concatenate([w_in, _rot_partner(w_in[:, Q_LORA + KV_LORA:])], axis=1).astype(BF16)
            wq3 = mla_w_uq[j].reshape(Q_LORA, MLA_HEADS, QK_NOPE + QK_ROPE)
            wq_ext = jnp.concatenate([wq3, _rot_partner(wq3[..., QK_NOPE:])], axis=-1)
            wq_ext = wq_ext.reshape(Q_LORA, MLA_HEADS * (QK_NOPE + 2 * QK_ROPE)).astype(BF16)
            w_kv = cast_layer(mla_w_ukv, j)
            wo = cast_layer(mla_wo, j)
            zl = matmul(hl, w_in, out_dtype=F32, bn=w_in.shape[1])
            zc = matmul(hc, w_in, out_dtype=F32, bn=w_in.shape[1])
            q = q_proj(zl, mla_q_g[j], wq_ext, cs_lat, scale=q_scale)
            kv_l, kpe_l = kv_proj(zl, mla_kv_g[j], w_kv, cs_lat)
            kv_c, kpe_c = kv_proj(zc, mla_kv_g[j], w_kv, cs_ctx)
            o = mla_attention(q, kv_c, kpe_c, kv_l, kpe_l, batch=b, heads=MLA_HEADS)
            yl = matmul(o, wo)
            if not last:
                raise NotImplementedError("context output of an MLA layer that is not last")
        wq = cast_layer(peer_wq, i)
        kk = jnp.stack([peer_k1[i], peer_k2[i]], axis=1).reshape(2 * PEER_HEADS, N_KEYS, PEER_HALF)
        u = cast_layer(peer_u, i)
        v = cast_layer(peer_v, i)
        xl, h2l = norm_block(xl, norm2_g[i], sel_lat, res=(yl, mod, 2), mod=(mod, (3, 4)), write_x=True)
        pend_l = (peer(h2l, wq, kk, u, v), mod, 5)
        if not last:
            xc, h2c = norm_block(xc, norm2_g[i], sel_ctx, res=(yc, mod, 2), mod=(mod, (3, 4)), write_x=True)
            pend_c = (peer(h2c, wq, kk, u, v), mod, 5)

    out = norm_block(xl, final_g, sel_lat, res=pend_l, out_dtype=F32)
    return out.reshape(b, n, d)
```

```python
import functools
import math

import numpy as np
import jax
import jax.numpy as jnp
from jax import lax
from jax.experimental import pallas as pl
from jax.experimental.pallas import tpu as pltpu

F32 = jnp.float32
BF16 = jnp.bfloat16

EPS = 1e-6
ADA_CHUNKS = 6
F_GROUPS = 8
GRID_W = 64
MLA_HEADS = 32
QK_NOPE = 128
QK_ROPE = 64
V_DIM = 128
Q_LORA = 1024
KV_LORA = 512
ROPE_THETA = 10000.0
PEER_HEADS = 8
PEER_HALF = 128
N_KEYS = 128
PEER_TOPK = 16

LANES = 128
SUBLANES = 8
VMEM_LIMIT = 56 * 1024 * 1024
NEG_INF = float("-inf")
MAX_EXCESS = 64.0


def _params(*sem):
    return pltpu.CompilerParams(dimension_semantics=sem, vmem_limit_bytes=VMEM_LIMIT)


def _ada_kernel(ct_ref, w_ref, b_ref, o_ref, sb_ref, *, kc, rows):
    d, bn = w_ref.shape
    nt = bn // LANES

    @pl.when(pl.program_id(0) == 0)
    def _():
        c = ct_ref[...]
        s = c * jax.nn.sigmoid(c)
        for r in range(rows):
            sb_ref[r] = jnp.broadcast_to(s[:, r:r + 1], (d, LANES))

    def body(i, accs):
        k0 = pl.multiple_of(i * kc, kc)
        out = list(accs)
        for t in range(nt):
            wv = w_ref[pl.ds(k0, kc), t * LANES:(t + 1) * LANES]
            for r in range(rows):
                p = wv * sb_ref[r, pl.ds(k0, kc), :]
                out[r * nt + t] = out[r * nt + t] + p.reshape(kc // SUBLANES, SUBLANES, LANES).sum(axis=0)
        return tuple(out)

    init = tuple(jnp.zeros((SUBLANES, LANES), F32) for _ in range(rows * nt))
    accs = lax.fori_loop(0, d // kc, body, init)
    o_ref[...] = jnp.zeros_like(o_ref)
    for r in range(rows):
        for t in range(nt):
            row = accs[r * nt + t].sum(axis=0, keepdims=True) + b_ref[:, t * LANES:(t + 1) * LANES]
            o_ref[r:r + 1, t * LANES:(t + 1) * LANES] = row


def ada_linear(ct, w, b, layer, *, rows, bn=512, kc=256):
    nl, d, n = w.shape
    bn = min(bn, n)
    kc = min(kc, d)
    return pl.pallas_call(
        functools.partial(_ada_kernel, kc=kc, rows=rows),
        grid=(n // bn,),
        in_specs=[
            pl.BlockSpec((d, SUBLANES), lambda j: (0, 0)),
            pl.BlockSpec((None, d, bn), lambda j: (layer, 0, j)),
            pl.BlockSpec((None, 1, bn), lambda j: (layer, 0, j)),
        ],
        out_specs=pl.BlockSpec((SUBLANES, bn), lambda j: (0, j)),
        out_shape=jax.ShapeDtypeStruct((SUBLANES, n), F32),
        scratch_shapes=[pltpu.VMEM((rows, d, LANES), F32)],
        compiler_params=_params("arbitrary"),
        name="ada_linear",
    )(ct, w, b.reshape(nl, 1, n))


def _cast_kernel(w_ref, o_ref):
    o_ref[...] = w_ref[...].astype(o_ref.dtype)


def cast_layer(w, layer, *, bm=512):
    _, r, c = w.shape
    bm = min(bm, r)
    return pl.pallas_call(
        _cast_kernel,
        grid=(r // bm,),
        in_specs=[pl.BlockSpec((None, bm, c), lambda i: (layer, i, 0))],
        out_specs=pl.BlockSpec((bm, c), lambda i: (i, 0)),
        out_shape=jax.ShapeDtypeStruct((r, c), BF16),
        compiler_params=_params("parallel"),
        name="cast_layer",
    )(w)


def _norm_kernel(*refs, has_res, has_mod, write_x, pair_rows):
    it = iter(refs)
    x_ref = next(it)
    if has_res:
        y_ref, gate_ref = next(it), next(it)
    g_ref = next(it)
    if has_mod:
        shift_ref, scale_ref = next(it), next(it)
    if write_x:
        xo_ref = next(it)
    h_ref = next(it)

    def normed(x):
        y = x * lax.rsqrt(jnp.mean(x * x, axis=-1, keepdims=True) + EPS)
        y = y * g_ref[...]
        if has_mod:
            y = y * (1.0 + scale_ref[...]) + shift_ref[...]
        return y.astype(h_ref.dtype)

    if pair_rows:
        bm, d = x_ref.shape
        h = normed(x_ref[...])
        r = lax.broadcasted_iota(jnp.int32, (bm // 2, bm), 0)
        c = lax.broadcasted_iota(jnp.int32, (bm // 2, bm), 1)
        for p in range(2):
            pick = jnp.where(c == 2 * r + p, 1.0, 0.0).astype(h.dtype)
            h_ref[:, p * d:(p + 1) * d] = jnp.dot(pick, h, preferred_element_type=F32).astype(h_ref.dtype)
        return

    x = x_ref[...]
    if has_res:
        x = x + gate_ref[...] * y_ref[...].astype(F32)
    if write_x:
        xo_ref[...] = x
    h_ref[...] = normed(x)


def norm_block(x, g, sel, *, res=None, mod=None, write_x=False, pair_rows=False, out_dtype=BF16, bm=256):
    t, d = x.shape
    bm = min(bm, t)
    assert not (pair_rows and (res is not None or write_x))
    row = pl.BlockSpec((bm, d), lambda i: (i, 0))
    h_shape, h_block = ((t // 2, 2 * d), (bm // 2, 2 * d)) if pair_rows else ((t, d), (bm, d))

    def mod_spec(a):
        return pl.BlockSpec((None, None, 1, d), lambda i, a=a: (sel(i, bm), a, 0, 0))

    args, specs = [x], [row]
    if res is not None:
        args += [res[0], res[1]]
        specs += [row, mod_spec(res[2])]
    args.append(g.reshape(1, d))
    specs.append(pl.BlockSpec((1, d), lambda i: (0, 0)))
    if mod is not None:
        args += [mod[0], mod[0]]
        specs += [mod_spec(mod[1][0]), mod_spec(mod[1][1])]
    out_shape, out_specs = [], []
    if write_x:
        out_shape.append(jax.ShapeDtypeStruct((t, d), F32))
        out_specs.append(row)
    out_shape.append(jax.ShapeDtypeStruct(h_shape, out_dtype))
    out_specs.append(pl.BlockSpec(h_block, lambda i: (i, 0)))
    out = pl.pallas_call(
        functools.partial(_norm_kernel, has_res=res is not None, has_mod=mod is not None, write_x=write_x,
                          pair_rows=pair_rows),
        grid=(t // bm,),
        in_specs=specs,
        out_specs=out_specs,
        out_shape=out_shape,
        compiler_params=_params("parallel"),
        name="norm_block",
    )(*args)
    return out if write_x else out[0]


def _mm_kernel(a_ref, b_ref, o_ref):
    o_ref[...] = jnp.dot(a_ref[...], b_ref[...], preferred_element_type=F32).astype(o_ref.dtype)


def matmul(a, b, *, out_dtype=BF16, bm=512, bn=1024):
    m, k = a.shape
    _, n = b.shape
    bm, bn = min(bm, m), min(bn, n)
    return pl.pallas_call(
        _mm_kernel,
        grid=(n // bn, m // bm),
        in_specs=[pl.BlockSpec((bm, k), lambda j, i: (i, 0)),
                  pl.BlockSpec((k, bn), lambda j, i: (0, j))],
        out_specs=pl.BlockSpec((bm, bn), lambda j, i: (i, j)),
        out_shape=jax.ShapeDtypeStruct((m, n), out_dtype),
        compiler_params=_params("parallel", "parallel"),
        name="matmul",
    )(a, b)


def _dft_mats(n):
    k = np.arange(n, dtype=np.int64)
    ang = 2.0 * np.pi * ((k[:, None] * k[None, :]) % n).astype(np.float64) / n
    s = 1.0 / math.sqrt(n)
    return (np.cos(ang) * s).astype(np.float32), (np.sin(ang) * s).astype(np.float32)


def _chan_dft_kernel(h_ref, w_ref, o_ref):
    gd = h_ref.shape[1]
    y = jnp.dot(h_ref[...], w_ref[...], preferred_element_type=F32)
    o_ref[0] = y[:, :gd].astype(o_ref.dtype)
    o_ref[1] = y[:, gd:].astype(o_ref.dtype)


def chan_dft(h, wcs, *, bm=512):
    t, d = h.shape
    gd = wcs.shape[0]
    bm = min(bm, t)
    return pl.pallas_call(
        _chan_dft_kernel,
        grid=(t // bm, d // gd),
        in_specs=[pl.BlockSpec((bm, gd), lambda i, g: (i, g)),
                  pl.BlockSpec((gd, 2 * gd), lambda i, g: (0, 0))],
        out_specs=pl.BlockSpec((2, bm, gd), lambda i, g: (0, i, g)),
        out_shape=jax.ShapeDtypeStruct((2, t, d), BF16),
        compiler_params=_params("parallel", "parallel"),
        name="chan_dft",
    )(h, wcs)


def _pos_dft_mats(n):
    cn, sn = _dft_mats(n)
    half = n // 2
    me = np.concatenate([cn[:half, 0::2], -sn[:half, 0::2]], axis=1)
    mo = np.concatenate([cn[:half, 1::2], -sn[:half, 1::2]], axis=1)
    return me, mo


def _pos_dft_kernel(me_ref, mo_ref, yce_ref, yse_ref, yco_ref, yso_ref, o_ref):
    ye = jnp.concatenate([yce_ref[...], yse_ref[...]], axis=0)
    yo = jnp.concatenate([yco_ref[...], yso_ref[...]], axis=0)
    e = jnp.dot(me_ref[...], ye, preferred_element_type=F32)
    o = jnp.dot(mo_ref[...], yo, preferred_element_type=F32)
    o_ref[0] = (e + o).astype(o_ref.dtype)
    o_ref[1] = (e - o).astype(o_ref.dtype)


def pos_dft(me, mo, y2, n_seq, *, bm=512, bn=512):
    half, n = me.shape
    _, t2, d2 = y2.shape
    t, d = 2 * t2, d2 // 2
    bm, bn = min(bm, half), min(bn, d)
    nb = d // bn
    out = pl.pallas_call(
        _pos_dft_kernel,
        grid=(n_seq, nb, half // bm),
        in_specs=[pl.BlockSpec((bm, n), lambda b, j, i: (i, 0)),
                  pl.BlockSpec((bm, n), lambda b, j, i: (i, 0)),
                  pl.BlockSpec((None, half, bn), lambda b, j, i: (0, b, j)),
                  pl.BlockSpec((None, half, bn), lambda b, j, i: (1, b, j)),
                  pl.BlockSpec((None, half, bn), lambda b, j, i: (0, b, nb + j)),
                  pl.BlockSpec((None, half, bn), lambda b, j, i: (1, b, nb + j))],
        out_specs=pl.BlockSpec((None, 2, bm, bn), lambda b, j, i: (b, 0, i, j)),
        out_shape=jax.ShapeDtypeStruct((n_seq, 2, half, d), BF16),
        compiler_params=_params("parallel", "parallel", "parallel"),
        name="pos_dft",
    )(me, mo, y2, y2, y2, y2)
    return out.reshape(t, d)


def _rms_prologue(z, g):
    zn = z * lax.rsqrt(jnp.mean(z * z, axis=-1, keepdims=True) + EPS)
    return (zn * g).astype(BF16)


def _q_proj_kernel(z_ref, g_ref, w_ref, cs_ref, o_ref, zn_ref, *, scale):
    @pl.when(pl.program_id(1) == 0)
    def _():
        zn_ref[...] = _rms_prologue(z_ref[:, :Q_LORA], g_ref[...])

    acc = jnp.dot(zn_ref[...], w_ref[...], preferred_element_type=F32)
    cs = cs_ref[...]
    hw = QK_NOPE + 2 * QK_ROPE
    for h in range(w_ref.shape[1] // hw):
        o_ref[:, h * hw:h * hw + QK_NOPE] = (acc[:, h * hw:h * hw + QK_NOPE] * scale).astype(o_ref.dtype)
        t = acc[:, h * hw + QK_NOPE:(h + 1) * hw] * cs
        r = (t + pltpu.roll(t, QK_ROPE, axis=1)) * scale
        o_ref[:, h * hw + QK_NOPE:(h + 1) * hw] = r.astype(o_ref.dtype)


def q_proj(z, q_g, w_q, cs_tab, *, scale, bm=512, bn=1024):
    t, zw = z.shape
    n = w_q.shape[1]
    bm = min(bm, t)
    return pl.pallas_call(
        functools.partial(_q_proj_kernel, scale=scale),
        grid=(t // bm, n // bn),
        in_specs=[pl.BlockSpec((bm, zw), lambda i, j: (i, 0)),
                  pl.BlockSpec((1, Q_LORA), lambda i, j: (0, 0)),
                  pl.BlockSpec((Q_LORA, bn), lambda i, j: (0, j)),
                  pl.BlockSpec((bm, 2 * QK_ROPE), lambda i, j: (i, 0))],
        out_specs=pl.BlockSpec((bm, bn), lambda i, j: (i, j)),
        out_shape=jax.ShapeDtypeStruct((t, n), BF16),
        scratch_shapes=[pltpu.VMEM((bm, Q_LORA), BF16)],
        compiler_params=_params("parallel", "arbitrary"),
        name="q_proj",
    )(z, q_g.reshape(1, Q_LORA), w_q, cs_tab)


def _kv_proj_kernel(z_ref, g_ref, w_ref, cs_ref, o_ref, kpe_ref, zn_ref):
    @pl.when(pl.program_id(1) == 0)
    def _():
        zn_ref[...] = _rms_prologue(z_ref[:, Q_LORA:Q_LORA + KV_LORA], g_ref[...])
        t = z_ref[:, Q_LORA + KV_LORA:] * cs_ref[...]
        r = t + pltpu.roll(t, QK_ROPE, axis=1)
        lane = lax.broadcasted_iota(jnp.int32, r.shape, 1)
        kpe_ref[...] = jnp.where(lane < QK_ROPE, r, 0.0).astype(kpe_ref.dtype)

    o_ref[...] = jnp.dot(zn_ref[...], w_ref[...], preferred_element_type=F32).astype(o_ref.dtype)


def kv_proj(z, kv_g, w_kv, cs_tab, *, bm=512, bn=1024):
    t, zw = z.shape
    n = w_kv.shape[1]
    bm = min(bm, t)
    return pl.pallas_call(
        _kv_proj_kernel,
        grid=(t // bm, n // bn),
        in_specs=[pl.BlockSpec((bm, zw), lambda i, j: (i, 0)),
                  pl.BlockSpec((1, KV_LORA), lambda i, j: (0, 0)),
                  pl.BlockSpec((KV_LORA, bn), lambda i, j: (0, j)),
                  pl.BlockSpec((bm, 2 * QK_ROPE), lambda i, j: (i, 0))],
        out_specs=[pl.BlockSpec((bm, bn), lambda i, j: (i, j)),
                   pl.BlockSpec((bm, 2 * QK_ROPE), lambda i, j: (i, 0))],
        out_shape=[jax.ShapeDtypeStruct((t, n), BF16),
                   jax.ShapeDtypeStruct((t, 2 * QK_ROPE), BF16)],
        scratch_shapes=[pltpu.VMEM((bm, KV_LORA), BF16)],
        compiler_params=_params("parallel", "arbitrary"),
        name="kv_proj",
    )(z, kv_g.reshape(1, KV_LORA), w_kv, cs_tab)


def _attn_kernel(q_ref, kc_ref, pc_ref, vc_ref, kl_ref, pl_ref, vl_ref, o_ref, *, tq, tk):
    n = q_ref.shape[0]
    n_lat = kl_ref.shape[0]
    nt = (((1,), (1,)), ((), ()))

    def key_chunks():
        yield (jnp.concatenate([kc_ref[...], pc_ref[...]], axis=1), vc_ref[...])
        for c in range(n_lat // tk):
            rows = slice(c * tk, (c + 1) * tk)
            yield (jnp.concatenate([kl_ref[rows, :], pl_ref[rows, :]], axis=1), vl_ref[rows, :])

    def lane_max(s):
        m = s[:, :LANES]
        for j in range(1, s.shape[1] // LANES):
            m = jnp.maximum(m, s[:, j * LANES:(j + 1) * LANES])
        return m

    def one_pass(q):
        ref = None
        acc = None
        top = None
        for k, v in key_chunks():
            s = lax.dot_general(q, k, nt, preferred_element_type=F32)
            if ref is None:
                ref = jnp.max(s, axis=-1, keepdims=True)
            p = jnp.exp2(s - ref).astype(BF16)
            v1 = jnp.concatenate([v, jnp.ones_like(v)], axis=1)
            part = jnp.dot(p, v1, preferred_element_type=F32)
            acc = part if acc is None else acc + part
            m = lane_max(s)
            top = m if top is None else jnp.maximum(top, m)
        return acc[:, :V_DIM] / acc[:, V_DIM:], jnp.max(top - ref)

    def online(q):
        m = jnp.full((tq, 1), NEG_INF, F32)
        l = jnp.zeros((tq, 1), F32)
        acc = jnp.zeros((tq, V_DIM), F32)
        for k, v in key_chunks():
            s = lax.dot_general(q, k, nt, preferred_element_type=F32)
            m_new = jnp.maximum(m, jnp.max(s, axis=-1, keepdims=True))
            alpha = jnp.exp2(m - m_new)
            p = jnp.exp2(s - m_new)
            l = alpha * l + jnp.sum(p, axis=-1, keepdims=True)
            acc = alpha * acc + jnp.dot(p.astype(BF16), v, preferred_element_type=F32)
            m = m_new
        return acc / l

    def q_tile(i, carry):
        r0 = pl.multiple_of(i * tq, tq)
        q = q_ref[pl.ds(r0, tq), :]
        out, excess = one_pass(q)
        o_ref[pl.ds(r0, tq), :] = out.astype(o_ref.dtype)

        @pl.when(jnp.logical_not(excess <= MAX_EXCESS))
        def _():
            o_ref[pl.ds(r0, tq), :] = online(q).astype(o_ref.dtype)

        return carry

    lax.fori_loop(0, n // tq, q_tile, 0)


def mla_attention(q, kv_c, kpe_c, kv_l, kpe_l, *, batch, heads, tq=512, tk=512):
    n = q.shape[0] // batch
    n_ctx = kv_c.shape[0] // batch
    tq, tk = min(tq, n), min(tk, n)
    hq = QK_NOPE + 2 * QK_ROPE
    return pl.pallas_call(
        functools.partial(_attn_kernel, tq=tq, tk=tk),
        grid=(batch, heads),
        in_specs=[pl.BlockSpec((n, hq), lambda b, h: (b, h)),
                  pl.BlockSpec((n_ctx, QK_NOPE), lambda b, h: (b, 2 * h)),
                  pl.BlockSpec((n_ctx, 2 * QK_ROPE), lambda b, h: (b, 0)),
                  pl.BlockSpec((n_ctx, V_DIM), lambda b, h: (b, 2 * h + 1)),
                  pl.BlockSpec((n, QK_NOPE), lambda b, h: (b, 2 * h)),
                  pl.BlockSpec((n, 2 * QK_ROPE), lambda b, h: (b, 0)),
                  pl.BlockSpec((n, V_DIM), lambda b, h: (b, 2 * h + 1))],
        out_specs=pl.BlockSpec((n, V_DIM), lambda b, h: (b, h)),
        out_shape=jax.ShapeDtypeStruct((batch * n, heads * V_DIM), BF16),
        compiler_params=_params("parallel", "parallel"),
        name="mla_attention",
    )(q, kv_c, kpe_c, kv_c, kv_l, kpe_l, kv_l)


def _peer_scores_kernel(h_ref, wq_ref, kk_ref, o_ref):
    q = jnp.dot(h_ref[...], wq_ref[...], preferred_element_type=F32)
    nt = (((1,), (1,)), ((), ()))
    for c in range(wq_ref.shape[1] // PEER_HALF):
        qc = q[:, c * PEER_HALF:(c + 1) * PEER_HALF]
        o_ref[c] = lax.dot_general(kk_ref[c], qc, nt, precision=lax.Precision.HIGHEST,
                                   preferred_element_type=F32)


def peer_scores(h, wq, kk, *, bm=512, bn=1024):
    t, d = h.shape
    n = wq.shape[1]
    bm, bn = min(bm, t), min(bn, n)
    cb = bn // PEER_HALF
    return pl.pallas_call(
        _peer_scores_kernel,
        grid=(n // bn, t // bm),
        in_specs=[pl.BlockSpec((bm, d), lambda j, i: (i, 0)),
                  pl.BlockSpec((d, bn), lambda j, i: (0, j)),
                  pl.BlockSpec((cb, N_KEYS, PEER_HALF), lambda j, i: (j, 0, 0))],
        out_specs=pl.BlockSpec((cb, N_KEYS, bm), lambda j, i: (j, 0, i)),
        out_shape=jax.ShapeDtypeStruct((n // PEER_HALF, N_KEYS, t), F32),
        compiler_params=_params("parallel", "parallel"),
        name="peer_scores",
    )(h, wq, kk)


_HALF_K = PEER_TOPK // 2


def _top16(s, iota):
    vals = []
    rank = jnp.full(s.shape, 255, jnp.int32)
    for r in range(PEER_TOPK):
        m = jnp.max(s, axis=0, keepdims=True)
        idx = jnp.min(jnp.where(s == m, iota, N_KEYS), axis=0, keepdims=True)
        hit = iota == idx
        rank = jnp.where(hit, r, rank)
        s = jnp.where(hit, NEG_INF, s)
        vals.append(m)
    return vals, rank


def _extract16(s):
    vals = []
    for _ in range(PEER_TOPK):
        m = jnp.max(s, axis=0, keepdims=True)
        vals.append(m)
        s = jnp.where(s == m, NEG_INF, s)
    removed = jnp.sum(jnp.where(s == NEG_INF, 1.0, 0.0), axis=0, keepdims=True)
    return vals, removed != float(PEER_TOPK)


def _candidates(a, b):
    a_hi = jnp.concatenate(a[_HALF_K:], axis=0)
    b_all = jnp.concatenate(b, axis=0)
    b_lo = b_all[:_HALF_K]
    cand = jnp.concatenate([a[0] + b_all] + [a[r] + b_lo for r in range(1, _HALF_K)]
                           + [a_hi + b[0]], axis=0)
    return cand, b_all, b_lo


def _peer_topk_fast(s_ref, k1_ref, k2_ref, h):
    tb = s_ref.shape[2]
    s1 = s_ref[2 * h]
    s2 = s_ref[2 * h + 1]
    a, bad1 = _extract16(s1)
    b, bad2 = _extract16(s2)
    cand, b_all, b_lo = _candidates(a, b)
    sums, bad3 = _extract16(cand)
    c16 = sums[-1]
    keep = cand >= c16
    z = jnp.sum(jnp.where(keep, jnp.exp(cand - sums[0]), 0.0), axis=0, keepdims=True)
    inf = float("inf")
    thr = [jnp.min(jnp.where(keep[:PEER_TOPK], b_all, inf), axis=0, keepdims=True)]
    for r in range(1, _HALF_K):
        lo = PEER_TOPK + (r - 1) * _HALF_K
        thr.append(jnp.min(jnp.where(keep[lo:lo + _HALF_K], b_lo, inf), axis=0, keepdims=True))
    lo = PEER_TOPK + (_HALF_K - 1) * _HALF_K
    thr += [jnp.where(keep[lo + r:lo + r + 1], b[0], inf) for r in range(_HALF_K)]
    th = jnp.full((N_KEYS, tb), inf, F32)
    for r in range(PEER_TOPK):
        th = jnp.where(s1 == a[r], thr[r], th)
    k1_ref[h, 0] = th
    k1_ref[h, 1] = jnp.exp(s1 - a[0]) * (1.0 / z)
    k2_ref[h, 0] = s2
    k2_ref[h, 1] = jnp.exp(s2 - b[0])
    return bad1 | bad2 | bad3


def _peer_topk_kernel(s_ref, k1_ref, k2_ref):
    for h in range(s_ref.shape[0] // 2):
        bad = _peer_topk_fast(s_ref, k1_ref, k2_ref, h)

        @pl.when(jnp.max(jnp.where(bad, 1.0, 0.0)) > 0.0)
        def _():
            _peer_topk_exact(s_ref, k1_ref, k2_ref, h)


def _peer_topk_exact(s_ref, k1_ref, k2_ref, h):
    tb = s_ref.shape[2]
    iota = lax.broadcasted_iota(jnp.int32, (N_KEYS, tb), 0)
    i16 = lax.broadcasted_iota(jnp.int32, (PEER_TOPK, tb), 0)
    i8 = lax.broadcasted_iota(jnp.int32, (_HALF_K, tb), 0)
    flat = jnp.concatenate([i16] + [i8 + PEER_TOPK * r for r in range(1, _HALF_K)]
                           + [(i8 + _HALF_K) * PEER_TOPK], axis=0)
    s1 = s_ref[2 * h]
    s2 = s_ref[2 * h + 1]
    a, rank1 = _top16(s1, iota)
    b, rank2 = _top16(s2, iota)
    cand, _, _ = _candidates(a, b)
    cmax = a[0] + b[0]
    sel = jnp.zeros(flat.shape, jnp.bool_)
    c = cand
    for _ in range(PEER_TOPK):
        m = jnp.max(c, axis=0, keepdims=True)
        idx = jnp.min(jnp.where(c == m, flat, PEER_TOPK * PEER_TOPK), axis=0, keepdims=True)
        hit = flat == idx
        sel = jnp.logical_or(sel, hit)
        c = jnp.where(hit, NEG_INF, c)
    self = sel.astype(F32)
    z = jnp.sum(self * jnp.exp(cand - cmax), axis=0, keepdims=True)
    counts = [jnp.sum(self[:PEER_TOPK], axis=0, keepdims=True)]
    for r in range(1, _HALF_K):
        lo = PEER_TOPK + (r - 1) * _HALF_K
        counts.append(jnp.sum(self[lo:lo + _HALF_K], axis=0, keepdims=True))
    tail = self[PEER_TOPK + (_HALF_K - 1) * _HALF_K:]
    counts += [tail[r:r + 1] for r in range(_HALF_K)]
    lim = jnp.zeros((N_KEYS, tb), F32)
    for r in range(PEER_TOPK):
        lim = jnp.where(rank1 == r, counts[r], lim)
    k1_ref[h, 0] = 1.0 - lim
    k1_ref[h, 1] = jnp.exp(s1 - a[0]) * (1.0 / z)
    k2_ref[h, 0] = -rank2.astype(F32)
    k2_ref[h, 1] = jnp.exp(s2 - b[0])


def peer_topk(scores, *, tb=256):
    c2, nk, t = scores.shape
    tb = min(tb, t)
    hh = c2 // 2
    return pl.pallas_call(
        _peer_topk_kernel,
        grid=(t // tb,),
        in_specs=[pl.BlockSpec((c2, nk, tb), lambda i: (0, 0, i))],
        out_specs=[pl.BlockSpec((hh, 2, nk, tb), lambda i: (0, 0, 0, i)),
                   pl.BlockSpec((hh, 2, nk, tb), lambda i: (0, 0, 0, i))],
        out_shape=[jax.ShapeDtypeStruct((hh, 2, nk, t), F32),
                   jax.ShapeDtypeStruct((hh, 2, nk, t), F32)],
        compiler_params=_params("parallel"),
        name="peer_topk",
    )(scores)


def _gelu(x):
    return 0.5 * x * (1.0 + lax.erf(x * (1.0 / math.sqrt(2.0))))


def _peer_dense_kernel(h_ref, u_ref, v_ref, k1_ref, k2_ref, o_ref):
    j = pl.program_id(1)
    n1 = u_ref.shape[0] // N_KEYS
    tb = h_ref.shape[0]
    nt = (((1,), (1,)), ((), ()))
    tn = (((0,), (0,)), ((), ()))
    odd = (j % 2) == 1

    @pl.when(j == 0)
    def _():
        o_ref[...] = jnp.zeros_like(o_ref)

    a_t = lax.dot_general(u_ref[...], h_ref[...], nt, preferred_element_type=F32)
    parts = []
    for c in range(n1):
        g = jnp.zeros((N_KEYS, tb), F32)
        for h in range(k1_ref.shape[0]):
            lo, hi = k1_ref[h, :, c:c + 1, :], k1_ref[h, :, n1 + c:n1 + c + 1, :]
            thr = jnp.where(odd, hi[0], lo[0])
            p1 = jnp.where(odd, hi[1], lo[1])
            g = g + jnp.where(k2_ref[h, 0] >= thr, k2_ref[h, 1] * p1, 0.0)
        parts.append(g)
    gate = jnp.concatenate(parts, axis=0)
    w_t = (gate * _gelu(a_t)).astype(BF16)
    o_ref[...] += lax.dot_general(w_t, v_ref[...], tn, preferred_element_type=F32)


def peer_dense(h, u, v, key1, key2, *, tb=512, ec=512):
    t, d = h.shape
    e = u.shape[0]
    tb = min(tb, t)
    hh, _, nk, _ = key1.shape
    assert 2 * (ec // nk) == SUBLANES
    return pl.pallas_call(
        _peer_dense_kernel,
        grid=(t // tb, e // ec),
        in_specs=[pl.BlockSpec((tb, d), lambda i, j: (i, 0)),
                  pl.BlockSpec((ec, d), lambda i, j: (j, 0)),
                  pl.BlockSpec((ec, d), lambda i, j: (j, 0)),
                  pl.BlockSpec((hh, 2, SUBLANES, tb), lambda i, j: (0, 0, j // 2, i)),
                  pl.BlockSpec((hh, 2, nk, tb), lambda i, j: (0, 0, 0, i))],
        out_specs=pl.BlockSpec((tb, d), lambda i, j: (i, 0)),
        out_shape=jax.ShapeDtypeStruct((t, d), F32),
        compiler_params=_params("parallel", "arbitrary"),
        name="peer_dense",
    )(h, u, v, key1, key2)


def peer(h, wq, kk, u, v):
    scores = peer_scores(h, wq, kk)
    key1, key2 = peer_topk(scores)
    return peer_dense(h, u, v, key1, key2)


def _rope_tables(n):
    rows = n // GRID_W
    row_id = np.repeat(np.arange(rows), GRID_W).astype(np.float64)
    col_id = np.tile(np.arange(GRID_W), rows).astype(np.float64)
    axis_dim = QK_ROPE // 2
    inv_freq = ROPE_THETA ** (-np.arange(0, axis_dim, 2, dtype=np.float64) / axis_dim)
    ang = np.stack([row_id[:, None] * inv_freq, col_id[:, None] * inv_freq], axis=1)
    cos = np.cos(ang)
    sin = np.sin(ang)
    cos_full = np.concatenate([cos, cos], axis=-1).reshape(n, QK_ROPE)
    sin_full = np.concatenate([sin, sin], axis=-1).reshape(n, QK_ROPE)
    return np.concatenate([cos_full, sin_full], axis=-1).astype(np.float32)


def _rot_partner(w):
    ws = w.reshape(w.shape[:-1] + (2, 2, QK_ROPE // 4))
    return jnp.stack([-ws[..., 1, :], ws[..., 0, :]], axis=-2).reshape(w.shape)


def kernel(x, c, ctx, c_ctx, ada_w, ada_b, norm1_g, norm2_g, final_g, fnet_wo,
           mla_w_in, mla_q_g, mla_w_uq, mla_kv_g, mla_w_ukv, mla_wo,
           peer_wq, peer_k1, peer_k2, peer_u, peer_v):
    b, n, d = x.shape
    n_ctx = ctx.shape[1]
    depth = ada_w.shape[0]
    gd = d // F_GROUPS

    xl = x.reshape(b * n, d)
    xc = ctx.reshape(b * n_ctx, d)

    ct = jnp.zeros((d, SUBLANES), F32).at[:, :b].set(c.T).at[:, b].set(c_ctx)

    def sel_lat(i, bm):
        return (i * bm) // n

    def sel_ctx(i, bm):
        return b

    cc, sc = _dft_mats(gd)
    wcs = jnp.asarray(np.concatenate([cc, sc], axis=1)).astype(BF16)
    me_l, mo_l = (jnp.asarray(m).astype(BF16) for m in _pos_dft_mats(n))
    me_c, mo_c = (jnp.asarray(m).astype(BF16) for m in _pos_dft_mats(n_ctx))
    cs_lat = jnp.asarray(np.tile(_rope_tables(n), (b, 1)))
    cs_ctx = jnp.asarray(np.tile(np.concatenate([np.ones((1, QK_ROPE), np.float32),
                                                 np.zeros((1, QK_ROPE), np.float32)], axis=1),
                                 (b * n_ctx, 1)))
    q_scale = float((QK_NOPE + QK_ROPE) ** -0.5 * math.log2(math.e))

    pend_l = None
    pend_c = None
    for i in range(depth):
        last = i == depth - 1
        mod = ada_linear(ct, ada_w, ada_b, i, rows=b + 1)
        mod = mod.reshape(SUBLANES, ADA_CHUNKS, 1, d)

        fourier = i % 2 == 0
        if pend_l is None:
            hl = norm_block(xl, norm1_g[i], sel_lat, mod=(mod, (0, 1)), pair_rows=fourier)
            hc = norm_block(xc, norm1_g[i], sel_ctx, mod=(mod, (0, 1)), pair_rows=fourier)
        else:
            xl, hl = norm_block(xl, norm1_g[i], sel_lat, res=pend_l, mod=(mod, (0, 1)), write_x=True)
            xc, hc = norm_block(xc, norm1_g[i], sel_ctx, res=pend_c, mod=(mod, (0, 1)), write_x=True)
            if fourier:
                hl, hc = (h.reshape(h.shape[0] // 2, 2 * d) for h in (hl, hc))
        j = i // 2
        if fourier:
            wo = cast_layer(fnet_wo, j)
            fl = pos_dft(me_l, mo_l, chan_dft(hl, wcs), b)
            yl = matmul(fl, wo)
            if not last:
                fc = pos_dft(me_c, mo_c, chan_dft(hc, wcs), b)
                yc = matmul(fc, wo)
        else:
            w_in = mla_w_in[j]
            w_in = jnp.concatenate([w_in, _rot_partner(w_in[:, Q_LORA + KV_LORA:])], axis=1).astype(BF16)
            wq3 = mla_w_uq[j].reshape(Q_LORA, MLA_HEADS, QK_NOPE + QK_ROPE)
            wq_ext = jnp.concatenate([wq3, _rot_partner(wq3[..., QK_NOPE:])], axis=-1)
            wq_ext = wq_ext.reshape(Q_LORA, MLA_HEADS * (QK_NOPE + 2 * QK_ROPE)).astype(BF16)
            w_kv = cast_layer(mla_w_ukv, j)
            wo = cast_layer(mla_wo, j)
            zl = matmul(hl, w_in, out_dtype=F32, bn=w_in.shape[1])
            zc = matmul(hc, w_in, out_dtype=F32, bn=w_in.shape[1])
            q = q_proj(zl, mla_q_g[j], wq_ext, cs_lat, scale=q_scale)
            kv_l, kpe_l = kv_proj(zl, mla_kv_g[j], w_kv, cs_lat)
            kv_c, kpe_c = kv_proj(zc, mla_kv_g[j], w_kv, cs_ctx)
            o = mla_attention(q, kv_c, kpe_c, kv_l, kpe_l, batch=b, heads=MLA_HEADS)
            yl = matmul(o, wo)
            if not last:
                raise NotImplementedError("context output of an MLA layer that is not last")
        wq = cast_layer(peer_wq, i)
        kk = jnp.stack([peer_k1[i], peer_k2[i]], axis=1).reshape(2 * PEER_HEADS, N_KEYS, PEER_HALF)
        u = cast_layer(peer_u, i)
        v = cast_layer(peer_v, i)
        xl, h2l = norm_block(xl, norm2_g[i], sel_lat, res=(yl, mod, 2), mod=(mod, (3, 4)), write_x=True)
        pend_l = (peer(h2l, wq, kk, u, v), mod, 5)
        if not last:
            xc, h2c = norm_block(xc, norm2_g[i], sel_ctx, res=(yc, mod, 2), mod=(mod, (3, 4)), write_x=True)
            pend_c = (peer(h2c, wq, kk, u, v), mod, 5)

    out = norm_block(xl, final_g, sel_lat, res=pend_l, out_dtype=F32)
    return out.reshape(b, n, d)
```

```python
import functools
import math

import numpy as np
import jax
import jax.numpy as jnp
from jax import lax
from jax.experimental import pallas as pl
from jax.experimental.pallas import tpu as pltpu

F32 = jnp.float32
BF16 = jnp.bfloat16

EPS = 1e-6
ADA_CHUNKS = 6
F_GROUPS = 8
GRID_W = 64
MLA_HEADS = 32
QK_NOPE = 128
QK_ROPE = 64
V_DIM = 128
Q_LORA = 1024
KV_LORA = 512
ROPE_THETA = 10000.0
PEER_HEADS = 8
PEER_HALF = 128
N_KEYS = 128
PEER_TOPK = 16

LANES = 128
SUBLANES = 8
VMEM_LIMIT = 56 * 1024 * 1024
NEG_INF = float("-inf")
NO_RANK = 255.0
MAX_EXCESS = 64.0


def _params(*sem):
    return pltpu.CompilerParams(dimension_semantics=sem, vmem_limit_bytes=VMEM_LIMIT)


def _ada_kernel(ct_ref, w_ref, b_ref, o_ref, sb_ref, *, kc, rows):
    d, bn = w_ref.shape
    nt = bn // LANES

    @pl.when(pl.program_id(0) == 0)
    def _():
        c = ct_ref[...]
        s = c * jax.nn.sigmoid(c)
        for r in range(rows):
            sb_ref[r] = jnp.broadcast_to(s[:, r:r + 1], (d, LANES))

    def body(i, accs):
        k0 = pl.multiple_of(i * kc, kc)
        out = list(accs)
        for t in range(nt):
            wv = w_ref[pl.ds(k0, kc), t * LANES:(t + 1) * LANES]
            for r in range(rows):
                p = wv * sb_ref[r, pl.ds(k0, kc), :]
                out[r * nt + t] = out[r * nt + t] + p.reshape(kc // SUBLANES, SUBLANES, LANES).sum(axis=0)
        return tuple(out)

    init = tuple(jnp.zeros((SUBLANES, LANES), F32) for _ in range(rows * nt))
    accs = lax.fori_loop(0, d // kc, body, init)
    o_ref[...] = jnp.zeros_like(o_ref)
    for r in range(rows):
        for t in range(nt):
            row = accs[r * nt + t].sum(axis=0, keepdims=True) + b_ref[:, t * LANES:(t + 1) * LANES]
            o_ref[r:r + 1, t * LANES:(t + 1) * LANES] = row


def ada_linear(ct, w, b, layer, *, rows, bn=512, kc=256):
    nl, d, n = w.shape
    bn = min(bn, n)
    kc = min(kc, d)
    return pl.pallas_call(
        functools.partial(_ada_kernel, kc=kc, rows=rows),
        grid=(n // bn,),
        in_specs=[
            pl.BlockSpec((d, SUBLANES), lambda j: (0, 0)),
            pl.BlockSpec((None, d, bn), lambda j: (layer, 0, j)),
            pl.BlockSpec((None, 1, bn), lambda j: (layer, 0, j)),
        ],
        out_specs=pl.BlockSpec((SUBLANES, bn), lambda j: (0, j)),
        out_shape=jax.ShapeDtypeStruct((SUBLANES, n), F32),
        scratch_shapes=[pltpu.VMEM((rows, d, LANES), F32)],
        compiler_params=_params("arbitrary"),
        name="ada_linear",
    )(ct, w, b.reshape(nl, 1, n))


def _cast_kernel(w_ref, o_ref):
    o_ref[...] = w_ref[...].astype(o_ref.dtype)


def cast_layer(w, layer, *, bm=512):
    _, r, c = w.shape
    bm = min(bm, r)
    return pl.pallas_call(
        _cast_kernel,
        grid=(r // bm,),
        in_specs=[pl.BlockSpec((None, bm, c), lambda i: (layer, i, 0))],
        out_specs=pl.BlockSpec((bm, c), lambda i: (i, 0)),
        out_shape=jax.ShapeDtypeStruct((r, c), BF16),
        compiler_params=_params("parallel"),
        name="cast_layer",
    )(w)


def _norm_kernel(*refs, has_res, has_mod, write_x, pair_rows):
    it = iter(refs)
    x_ref = next(it)
    if has_res:
        y_ref, gate_ref = next(it), next(it)
    g_ref = next(it)
    if has_mod:
        shift_ref, scale_ref = next(it), next(it)
    if write_x:
        xo_ref = next(it)
    h_ref = next(it)

    def normed(x):
        y = x * lax.rsqrt(jnp.mean(x * x, axis=-1, keepdims=True) + EPS)
        y = y * g_ref[...]
        if has_mod:
            y = y * (1.0 + scale_ref[...]) + shift_ref[...]
        return y.astype(h_ref.dtype)

    if pair_rows:
        bm, d = x_ref.shape
        h = normed(x_ref[...])
        r = lax.broadcasted_iota(jnp.int32, (bm // 2, bm), 0)
        c = lax.broadcasted_iota(jnp.int32, (bm // 2, bm), 1)
        for p in range(2):
            pick = jnp.where(c == 2 * r + p, 1.0, 0.0).astype(h.dtype)
            h_ref[:, p * d:(p + 1) * d] = jnp.dot(pick, h, preferred_element_type=F32).astype(h_ref.dtype)
        return

    x = x_ref[...]
    if has_res:
        x = x + gate_ref[...] * y_ref[...].astype(F32)
    if write_x:
        xo_ref[...] = x
    h_ref[...] = normed(x)


def norm_block(x, g, sel, *, res=None, mod=None, write_x=False, pair_rows=False, out_dtype=BF16, bm=256):
    t, d = x.shape
    bm = min(bm, t)
    assert not (pair_rows and (res is not None or write_x))
    row = pl.BlockSpec((bm, d), lambda i: (i, 0))
    h_shape, h_block = ((t // 2, 2 * d), (bm // 2, 2 * d)) if pair_rows else ((t, d), (bm, d))

    def mod_spec(a):
        return pl.BlockSpec((None, None, 1, d), lambda i, a=a: (sel(i, bm), a, 0, 0))

    args, specs = [x], [row]
    if res is not None:
        args += [res[0], res[1]]
        specs += [row, mod_spec(res[2])]
    args.append(g.reshape(1, d))
    specs.append(pl.BlockSpec((1, d), lambda i: (0, 0)))
    if mod is not None:
        args += [mod[0], mod[0]]
        specs += [mod_spec(mod[1][0]), mod_spec(mod[1][1])]
    out_shape, out_specs = [], []
    if write_x:
        out_shape.append(jax.ShapeDtypeStruct((t, d), F32))
        out_specs.append(row)
    out_shape.append(jax.ShapeDtypeStruct(h_shape, out_dtype))
    out_specs.append(pl.BlockSpec(h_block, lambda i: (i, 0)))
    out = pl.pallas_call(
        functools.partial(_norm_kernel, has_res=res is not None, has_mod=mod is not None, write_x=write_x,
                          pair_rows=pair_rows),
        grid=(t // bm,),
        in_specs=specs,
        out_specs=out_specs,
        out_shape=out_shape,
        compiler_params=_params("parallel"),
        name="norm_block",
    )(*args)
    return out if write_x else out[0]


def _mm_kernel(a_ref, b_ref, o_ref):
    o_ref[...] = jnp.dot(a_ref[...], b_ref[...], preferred_element_type=F32).astype(o_ref.dtype)


def matmul(a, b, *, out_dtype=BF16, bm=512, bn=1024):
    m, k = a.shape
    _, n = b.shape
    bm, bn = min(bm, m), min(bn, n)
    return pl.pallas_call(
        _mm_kernel,
        grid=(n // bn, m // bm),
        in_specs=[pl.BlockSpec((bm, k), lambda j, i: (i, 0)),
                  pl.BlockSpec((k, bn), lambda j, i: (0, j))],
        out_specs=pl.BlockSpec((bm, bn), lambda j, i: (i, j)),
        out_shape=jax.ShapeDtypeStruct((m, n), out_dtype),
        compiler_params=_params("parallel", "parallel"),
        name="matmul",
    )(a, b)


def _dft_mats(n):
    k = np.arange(n, dtype=np.int64)
    ang = 2.0 * np.pi * ((k[:, None] * k[None, :]) % n).astype(np.float64) / n
    s = 1.0 / math.sqrt(n)
    return (np.cos(ang) * s).astype(np.float32), (np.sin(ang) * s).astype(np.float32)


def _chan_dft_kernel(h_ref, w_ref, o_ref):
    gd = h_ref.shape[1]
    y = jnp.dot(h_ref[...], w_ref[...], preferred_element_type=F32)
    o_ref[0] = y[:, :gd].astype(o_ref.dtype)
    o_ref[1] = y[:, gd:].astype(o_ref.dtype)


def chan_dft(h, wcs, *, bm=1024):
    t, d = h.shape
    gd = wcs.shape[0]
    bm = min(bm, t)
    return pl.pallas_call(
        _chan_dft_kernel,
        grid=(t // bm, d // gd),
        in_specs=[pl.BlockSpec((bm, gd), lambda i, g: (i, g)),
                  pl.BlockSpec((gd, 2 * gd), lambda i, g: (0, 0))],
        out_specs=pl.BlockSpec((2, bm, gd), lambda i, g: (0, i, g)),
        out_shape=jax.ShapeDtypeStruct((2, t, d), BF16),
        compiler_params=_params("parallel", "parallel"),
        name="chan_dft",
    )(h, wcs)


def _pos_dft_mats(n):
    cn, sn = _dft_mats(n)
    half = n // 2
    me = np.concatenate([cn[:half, 0::2], -sn[:half, 0::2]], axis=1)
    mo = np.concatenate([cn[:half, 1::2], -sn[:half, 1::2]], axis=1)
    return me, mo


def _pos_dft_kernel(me_ref, mo_ref, yce_ref, yse_ref, yco_ref, yso_ref, o_ref):
    ye = jnp.concatenate([yce_ref[...], yse_ref[...]], axis=0)
    yo = jnp.concatenate([yco_ref[...], yso_ref[...]], axis=0)
    e = jnp.dot(me_ref[...], ye, preferred_element_type=F32)
    o = jnp.dot(mo_ref[...], yo, preferred_element_type=F32)
    o_ref[0] = (e + o).astype(o_ref.dtype)
    o_ref[1] = (e - o).astype(o_ref.dtype)


def pos_dft(me, mo, y2, n_seq, *, bm=512, bn=512):
    half, n = me.shape
    _, t2, d2 = y2.shape
    t, d = 2 * t2, d2 // 2
    bm, bn = min(bm, half), min(bn, d)
    nb = d // bn
    out = pl.pallas_call(
        _pos_dft_kernel,
        grid=(n_seq, nb, half // bm),
        in_specs=[pl.BlockSpec((bm, n), lambda b, j, i: (i, 0)),
                  pl.BlockSpec((bm, n), lambda b, j, i: (i, 0)),
                  pl.BlockSpec((None, half, bn), lambda b, j, i: (0, b, j)),
                  pl.BlockSpec((None, half, bn), lambda b, j, i: (1, b, j)),
                  pl.BlockSpec((None, half, bn), lambda b, j, i: (0, b, nb + j)),
                  pl.BlockSpec((None, half, bn), lambda b, j, i: (1, b, nb + j))],
        out_specs=pl.BlockSpec((None, 2, bm, bn), lambda b, j, i: (b, 0, i, j)),
        out_shape=jax.ShapeDtypeStruct((n_seq, 2, half, d), BF16),
        compiler_params=_params("parallel", "parallel", "parallel"),
        name="pos_dft",
    )(me, mo, y2, y2, y2, y2)
    return out.reshape(t, d)


def _rms_prologue(z, g):
    zn = z * lax.rsqrt(jnp.mean(z * z, axis=-1, keepdims=True) + EPS)
    return (zn * g).astype(BF16)


def _q_proj_kernel(z_ref, g_ref, w_ref, cs_ref, o_ref, zn_ref):
    @pl.when(pl.program_id(1) == 0)
    def _():
        zn_ref[...] = _rms_prologue(z_ref[:, :Q_LORA], g_ref[...])

    acc = jnp.dot(zn_ref[...], w_ref[...], preferred_element_type=F32)
    cs = cs_ref[...]
    hw = QK_NOPE + 2 * QK_ROPE
    for h in range(w_ref.shape[1] // hw):
        o_ref[:, h * hw:h * hw + QK_NOPE] = acc[:, h * hw:h * hw + QK_NOPE].astype(o_ref.dtype)
        t = acc[:, h * hw + QK_NOPE:(h + 1) * hw] * cs
        r = t + pltpu.roll(t, QK_ROPE, axis=1)
        o_ref[:, h * hw + QK_NOPE:(h + 1) * hw] = r.astype(o_ref.dtype)


def q_proj(z, q_g, w_q, cs_tab, *, bm=1024, bn=2048):
    t, zw = z.shape
    n = w_q.shape[1]
    bm = min(bm, t)
    return pl.pallas_call(
        _q_proj_kernel,
        grid=(t // bm, n // bn),
        in_specs=[pl.BlockSpec((bm, zw), lambda i, j: (i, 0)),
                  pl.BlockSpec((1, Q_LORA), lambda i, j: (0, 0)),
                  pl.BlockSpec((Q_LORA, bn), lambda i, j: (0, j)),
                  pl.BlockSpec((bm, 2 * QK_ROPE), lambda i, j: (i, 0))],
        out_specs=pl.BlockSpec((bm, bn), lambda i, j: (i, j)),
        out_shape=jax.ShapeDtypeStruct((t, n), BF16),
        scratch_shapes=[pltpu.VMEM((bm, Q_LORA), BF16)],
        compiler_params=_params("parallel", "arbitrary"),
        name="q_proj",
    )(z, q_g.reshape(1, Q_LORA), w_q, cs_tab)


def _kv_proj_kernel(z_ref, g_ref, w_ref, cs_ref, o_ref, kpe_ref, zn_ref):
    @pl.when(pl.program_id(1) == 0)
    def _():
        zn_ref[...] = _rms_prologue(z_ref[:, Q_LORA:Q_LORA + KV_LORA], g_ref[...])
        t = z_ref[:, Q_LORA + KV_LORA:] * cs_ref[...]
        r = t + pltpu.roll(t, QK_ROPE, axis=1)
        lane = lax.broadcasted_iota(jnp.int32, r.shape, 1)
        kpe_ref[...] = jnp.where(lane < QK_ROPE, r, 0.0).astype(kpe_ref.dtype)

    o_ref[...] = jnp.dot(zn_ref[...], w_ref[...], preferred_element_type=F32).astype(o_ref.dtype)


def kv_proj(z, kv_g, w_kv, cs_tab, *, bm=1024, bn=2048):
    t, zw = z.shape
    n = w_kv.shape[1]
    bm = min(bm, t)
    return pl.pallas_call(
        _kv_proj_kernel,
        grid=(t // bm, n // bn),
        in_specs=[pl.BlockSpec((bm, zw), lambda i, j: (i, 0)),
                  pl.BlockSpec((1, KV_LORA), lambda i, j: (0, 0)),
                  pl.BlockSpec((KV_LORA, bn), lambda i, j: (0, j)),
                  pl.BlockSpec((bm, 2 * QK_ROPE), lambda i, j: (i, 0))],
        out_specs=[pl.BlockSpec((bm, bn), lambda i, j: (i, j)),
                   pl.BlockSpec((bm, 2 * QK_ROPE), lambda i, j: (i, 0))],
        out_shape=[jax.ShapeDtypeStruct((t, n), BF16),
                   jax.ShapeDtypeStruct((t, 2 * QK_ROPE), BF16)],
        scratch_shapes=[pltpu.VMEM((bm, KV_LORA), BF16)],
        compiler_params=_params("parallel", "arbitrary"),
        name="kv_proj",
    )(z, kv_g.reshape(1, KV_LORA), w_kv, cs_tab)


def _attn_kernel(q_ref, kc_ref, pc_ref, vc_ref, kl_ref, pl_ref, vl_ref, o_ref, *, tq, tk):
    n = q_ref.shape[0]
    n_lat = kl_ref.shape[0]
    nt = (((1,), (1,)), ((), ()))

    def key_chunks():
        yield (jnp.concatenate([kc_ref[...], pc_ref[...]], axis=1), vc_ref[...])
        for c in range(n_lat // tk):
            rows = slice(c * tk, (c + 1) * tk)
            yield (jnp.concatenate([kl_ref[rows, :], pl_ref[rows, :]], axis=1), vl_ref[rows, :])

    def lane_max(s):
        m = s[:, :LANES]
        for j in range(1, s.shape[1] // LANES):
            m = jnp.maximum(m, s[:, j * LANES:(j + 1) * LANES])
        return m

    def one_pass(q):
        ref = None
        acc = None
        top = None
        for k, v in key_chunks():
            s = lax.dot_general(q, k, nt, preferred_element_type=F32)
            if ref is None:
                ref = jnp.max(s, axis=-1, keepdims=True)
            p = jnp.exp2(s - ref).astype(BF16)
            v1 = jnp.concatenate([v, jnp.ones_like(v)], axis=1)
            part = jnp.dot(p, v1, preferred_element_type=F32)
            acc = part if acc is None else acc + part
            m = lane_max(s)
            top = m if top is None else jnp.maximum(top, m)
        return acc[:, :V_DIM] / acc[:, V_DIM:], jnp.max(top - ref)

    def online(q):
        m = jnp.full((tq, 1), NEG_INF, F32)
        l = jnp.zeros((tq, 1), F32)
        acc = jnp.zeros((tq, V_DIM), F32)
        for k, v in key_chunks():
            s = lax.dot_general(q, k, nt, preferred_element_type=F32)
            m_new = jnp.maximum(m, jnp.max(s, axis=-1, keepdims=True))
            alpha = jnp.exp2(m - m_new)
            p = jnp.exp2(s - m_new)
            l = alpha * l + jnp.sum(p, axis=-1, keepdims=True)
            acc = alpha * acc + jnp.dot(p.astype(BF16), v, preferred_element_type=F32)
            m = m_new
        return acc / l

    def q_tile(i, carry):
        r0 = pl.multiple_of(i * tq, tq)
        q = q_ref[pl.ds(r0, tq), :]
        out, excess = one_pass(q)
        o_ref[pl.ds(r0, tq), :] = out.astype(o_ref.dtype)

        @pl.when(jnp.logical_not(excess <= MAX_EXCESS))
        def _():
            o_ref[pl.ds(r0, tq), :] = online(q).astype(o_ref.dtype)

        return carry

    lax.fori_loop(0, n // tq, q_tile, 0)


def mla_attention(q, kv_c, kpe_c, kv_l, kpe_l, *, batch, heads, tq=512, tk=512):
    n = q.shape[0] // batch
    n_ctx = kv_c.shape[0] // batch
    tq, tk = min(tq, n), min(tk, n)
    hq = QK_NOPE + 2 * QK_ROPE
    return pl.pallas_call(
        functools.partial(_attn_kernel, tq=tq, tk=tk),
        grid=(batch, heads),
        in_specs=[pl.BlockSpec((n, hq), lambda b, h: (b, h)),
                  pl.BlockSpec((n_ctx, QK_NOPE), lambda b, h: (b, 2 * h)),
                  pl.BlockSpec((n_ctx, 2 * QK_ROPE), lambda b, h: (b, 0)),
                  pl.BlockSpec((n_ctx, V_DIM), lambda b, h: (b, 2 * h + 1)),
                  pl.BlockSpec((n, QK_NOPE), lambda b, h: (b, 2 * h)),
                  pl.BlockSpec((n, 2 * QK_ROPE), lambda b, h: (b, 0)),
                  pl.BlockSpec((n, V_DIM), lambda b, h: (b, 2 * h + 1))],
        out_specs=pl.BlockSpec((n, V_DIM), lambda b, h: (b, h)),
        out_shape=jax.ShapeDtypeStruct((batch * n, heads * V_DIM), BF16),
        compiler_params=_params("parallel", "parallel"),
        name="mla_attention",
    )(q, kv_c, kpe_c, kv_c, kv_l, kpe_l, kv_l)


def _peer_scores_kernel(h_ref, wq_ref, kk_ref, o_ref):
    q = jnp.dot(h_ref[...], wq_ref[...], preferred_element_type=F32)
    nt = (((1,), (1,)), ((), ()))
    for c in range(wq_ref.shape[1] // PEER_HALF):
        qc = q[:, c * PEER_HALF:(c + 1) * PEER_HALF]
        o_ref[c] = lax.dot_general(kk_ref[c], qc, nt, precision=lax.Precision.HIGHEST,
                                   preferred_element_type=F32)


def peer_scores(h, wq, kk, *, bm=512, bn=1024):
    t, d = h.shape
    n = wq.shape[1]
    bm, bn = min(bm, t), min(bn, n)
    cb = bn // PEER_HALF
    return pl.pallas_call(
        _peer_scores_kernel,
        grid=(n // bn, t // bm),
        in_specs=[pl.BlockSpec((bm, d), lambda j, i: (i, 0)),
                  pl.BlockSpec((d, bn), lambda j, i: (0, j)),
                  pl.BlockSpec((cb, N_KEYS, PEER_HALF), lambda j, i: (j, 0, 0))],
        out_specs=pl.BlockSpec((cb, N_KEYS, bm), lambda j, i: (j, 0, i)),
        out_shape=jax.ShapeDtypeStruct((n // PEER_HALF, N_KEYS, t), F32),
        compiler_params=_params("parallel", "parallel"),
        name="peer_scores",
    )(h, wq, kk)


_HALF_K = PEER_TOPK // 2


def _top16(s, iota):
    vals = []
    rank = jnp.full(s.shape, int(NO_RANK), jnp.int32)
    for r in range(PEER_TOPK):
        m = jnp.max(s, axis=0, keepdims=True)
        idx = jnp.min(jnp.where(s == m, iota, N_KEYS), axis=0, keepdims=True)
        hit = iota == idx
        rank = jnp.where(hit, r, rank)
        s = jnp.where(hit, NEG_INF, s)
        vals.append(m)
    return vals, rank


def _extract16(s, with_rank=False):
    vals = []
    rank = jnp.full(s.shape, NO_RANK, F32) if with_rank else None
    for r in range(PEER_TOPK):
        m = jnp.max(s, axis=0, keepdims=True)
        vals.append(m)
        hit = s == m
        if with_rank:
            rank = jnp.where(hit, float(r), rank)
        s = jnp.where(hit, NEG_INF, s)
    removed = jnp.sum(jnp.where(s == NEG_INF, 1.0, 0.0), axis=0, keepdims=True)
    return vals, removed != float(PEER_TOPK), rank


def _candidates(a, b):
    a_hi = jnp.concatenate(a[_HALF_K:], axis=0)
    b_all = jnp.concatenate(b, axis=0)
    b_lo = b_all[:_HALF_K]
    return jnp.concatenate([a[0] + b_all] + [a[r] + b_lo for r in range(1, _HALF_K)]
                           + [a_hi + b[0]], axis=0)


def _peer_topk_fast(s_ref, k1_ref, k2_ref, h):
    tb = s_ref.shape[2]
    s1 = s_ref[2 * h]
    s2 = s_ref[2 * h + 1]
    a, bad1, _ = _extract16(s1)
    b, bad2, rank2 = _extract16(s2, with_rank=True)
    cand = _candidates(a, b)
    sums, bad3, _ = _extract16(cand)
    keep = jnp.where(cand >= sums[-1], 1.0, 0.0)
    z = jnp.sum(keep * jnp.exp(cand - sums[0]), axis=0, keepdims=True)
    lim = jnp.zeros((N_KEYS, tb), F32)
    for r, count in enumerate(_rank1_counts(keep)):
        lim = jnp.where(s1 == a[r], count, lim)
    _emit(k1_ref, k2_ref, h, lim, rank2, jnp.exp(s1 - a[0]) * (1.0 / z), jnp.exp(s2 - b[0]))
    return bad1 | bad2 | bad3


def _rank1_counts(sel):
    counts = [jnp.sum(sel[:PEER_TOPK], axis=0, keepdims=True)]
    for r in range(1, _HALF_K):
        lo = PEER_TOPK + (r - 1) * _HALF_K
        counts.append(jnp.sum(sel[lo:lo + _HALF_K], axis=0, keepdims=True))
    tail = sel[PEER_TOPK + (_HALF_K - 1) * _HALF_K:]
    return counts + [tail[r:r + 1] for r in range(_HALF_K)]


def _emit(k1_ref, k2_ref, h, count, rank2, p1, p2):
    k1_ref[h, 0] = 1.0 - count
    k1_ref[h, 1] = p1
    k2_ref[h, 0] = (-rank2).astype(k2_ref.dtype)
    k2_ref[h, 1] = p2.astype(k2_ref.dtype)


def _peer_topk_kernel(s_ref, k1_ref, k2_ref):
    for h in range(s_ref.shape[0] // 2):
        bad = _peer_topk_fast(s_ref, k1_ref, k2_ref, h)

        @pl.when(jnp.max(jnp.where(bad, 1.0, 0.0)) > 0.0)
        def _():
            _peer_topk_exact(s_ref, k1_ref, k2_ref, h)


def _peer_topk_exact(s_ref, k1_ref, k2_ref, h):
    tb = s_ref.shape[2]
    iota = lax.broadcasted_iota(jnp.int32, (N_KEYS, tb), 0)
    i16 = lax.broadcasted_iota(jnp.int32, (PEER_TOPK, tb), 0)
    i8 = lax.broadcasted_iota(jnp.int32, (_HALF_K, tb), 0)
    flat = jnp.concatenate([i16] + [i8 + PEER_TOPK * r for r in range(1, _HALF_K)]
                           + [(i8 + _HALF_K) * PEER_TOPK], axis=0)
    s1 = s_ref[2 * h]
    s2 = s_ref[2 * h + 1]
    a, rank1 = _top16(s1, iota)
    b, rank2 = _top16(s2, iota)
    cand = _candidates(a, b)
    cmax = a[0] + b[0]
    sel = jnp.zeros(flat.shape, jnp.bool_)
    c = cand
    for _ in range(PEER_TOPK):
        m = jnp.max(c, axis=0, keepdims=True)
        idx = jnp.min(jnp.where(c == m, flat, PEER_TOPK * PEER_TOPK), axis=0, keepdims=True)
        hit = flat == idx
        sel = jnp.logical_or(sel, hit)
        c = jnp.where(hit, NEG_INF, c)
    self = sel.astype(F32)
    z = jnp.sum(self * jnp.exp(cand - cmax), axis=0, keepdims=True)
    lim = jnp.zeros((N_KEYS, tb), F32)
    for r, count in enumerate(_rank1_counts(self)):
        lim = jnp.where(rank1 == r, count, lim)
    _emit(k1_ref, k2_ref, h, lim, rank2.astype(F32), jnp.exp(s1 - a[0]) * (1.0 / z), jnp.exp(s2 - b[0]))


def peer_topk(scores, *, tb=256):
    c2, nk, t = scores.shape
    tb = min(tb, t)
    hh = c2 // 2
    return pl.pallas_call(
        _peer_topk_kernel,
        grid=(t // tb,),
        in_specs=[pl.BlockSpec((c2, nk, tb), lambda i: (0, 0, i))],
        out_specs=[pl.BlockSpec((hh, 2, nk, tb), lambda i: (0, 0, 0, i)),
                   pl.BlockSpec((hh, 2, nk, tb), lambda i: (0, 0, 0, i))],
        out_shape=[jax.ShapeDtypeStruct((hh, 2, nk, t), F32),
                   jax.ShapeDtypeStruct((hh, 2, nk, t), BF16)],
        compiler_params=_params("parallel"),
        name="peer_topk",
    )(scores)


def _gelu(x):
    return 0.5 * x * (1.0 + lax.erf(x * (1.0 / math.sqrt(2.0))))


def _peer_dense_kernel(h_ref, u_ref, v_ref, k1_ref, k2_ref, o_ref):
    j = pl.program_id(1)
    n1 = u_ref.shape[0] // N_KEYS
    tb = h_ref.shape[0]
    nt = (((1,), (1,)), ((), ()))
    tn = (((0,), (0,)), ((), ()))
    odd = (j % 2) == 1

    @pl.when(j == 0)
    def _():
        o_ref[...] = jnp.zeros_like(o_ref)

    a_t = lax.dot_general(u_ref[...], h_ref[...], nt, preferred_element_type=F32)
    parts = []
    for c in range(n1):
        g = jnp.zeros((N_KEYS, tb), BF16)
        for h in range(k1_ref.shape[0]):
            lo, hi = k1_ref[h, :, c:c + 1, :], k1_ref[h, :, n1 + c:n1 + c + 1, :]
            thr = jnp.broadcast_to(jnp.where(odd, hi[0], lo[0]), (N_KEYS, tb)).astype(BF16)
            p1 = jnp.broadcast_to(jnp.where(odd, hi[1], lo[1]), (N_KEYS, tb)).astype(BF16)
            g = g + jnp.where(k2_ref[h, 0] >= thr, k2_ref[h, 1] * p1, jnp.zeros((), BF16))
        parts.append(g)
    gate = jnp.concatenate(parts, axis=0)
    w_t = gate * _gelu(a_t).astype(BF16)
    o_ref[...] += lax.dot_general(w_t, v_ref[...], tn, preferred_element_type=F32)


def peer_dense(h, u, v, key1, key2, *, tb=512, ec=512):
    t, d = h.shape
    e = u.shape[0]
    tb = min(tb, t)
    hh, _, nk, _ = key1.shape
    assert 2 * (ec // nk) == SUBLANES
    return pl.pallas_call(
        _peer_dense_kernel,
        grid=(t // tb, e // ec),
        in_specs=[pl.BlockSpec((tb, d), lambda i, j: (i, 0)),
                  pl.BlockSpec((ec, d), lambda i, j: (j, 0)),
                  pl.BlockSpec((ec, d), lambda i, j: (j, 0)),
                  pl.BlockSpec((hh, 2, SUBLANES, tb), lambda i, j: (0, 0, j // 2, i)),
                  pl.BlockSpec((hh, 2, nk, tb), lambda i, j: (0, 0, 0, i))],
        out_specs=pl.BlockSpec((tb, d), lambda i, j: (i, 0)),
        out_shape=jax.ShapeDtypeStruct((t, d), F32),
        compiler_params=_params("parallel", "arbitrary"),
        name="peer_dense",
    )(h, u, v, key1, key2)


def peer(h, wq, kk, u, v):
    scores = peer_scores(h, wq, kk)
    key1, key2 = peer_topk(scores)
    return peer_dense(h, u, v, key1, key2)


def _rope_tables(n):
    rows = n // GRID_W
    row_id = np.repeat(np.arange(rows), GRID_W).astype(np.float64)
    col_id = np.tile(np.arange(GRID_W), rows).astype(np.float64)
    axis_dim = QK_ROPE // 2
    inv_freq = ROPE_THETA ** (-np.arange(0, axis_dim, 2, dtype=np.float64) / axis_dim)
    ang = np.stack([row_id[:, None] * inv_freq, col_id[:, None] * inv_freq], axis=1)
    cos = np.cos(ang)
    sin = np.sin(ang)
    cos_full = np.concatenate([cos, cos], axis=-1).reshape(n, QK_ROPE)
    sin_full = np.concatenate([sin, sin], axis=-1).reshape(n, QK_ROPE)
    return np.concatenate([cos_full, sin_full], axis=-1).astype(np.float32)


def _rot_partner(w):
    ws = w.reshape(w.shape[:-1] + (2, 2, QK_ROPE // 4))
    return jnp.stack([-ws[..., 1, :], ws[..., 0, :]], axis=-2).reshape(w.shape)


def kernel(x, c, ctx, c_ctx, ada_w, ada_b, norm1_g, norm2_g, final_g, fnet_wo,
           mla_w_in, mla_q_g, mla_w_uq, mla_kv_g, mla_w_ukv, mla_wo,
           peer_wq, peer_k1, peer_k2, peer_u, peer_v):
    b, n, d = x.shape
    n_ctx = ctx.shape[1]
    depth = ada_w.shape[0]
    gd = d // F_GROUPS

    xl = x.reshape(b * n, d)
    xc = ctx.reshape(b * n_ctx, d)

    ct = jnp.zeros((d, SUBLANES), F32).at[:, :b].set(c.T).at[:, b].set(c_ctx)

    def sel_lat(i, bm):
        return (i * bm) // n

    def sel_ctx(i, bm):
        return b

    cc, sc = _dft_mats(gd)
    wcs = jnp.asarray(np.concatenate([cc, sc], axis=1)).astype(BF16)
    me_l, mo_l = (jnp.asarray(m).astype(BF16) for m in _pos_dft_mats(n))
    me_c, mo_c = (jnp.asarray(m).astype(BF16) for m in _pos_dft_mats(n_ctx))
    cs_lat = jnp.asarray(np.tile(_rope_tables(n), (b, 1)))
    cs_ctx = jnp.asarray(np.tile(np.concatenate([np.ones((1, QK_ROPE), np.float32),
                                                 np.zeros((1, QK_ROPE), np.float32)], axis=1),
                                 (b * n_ctx, 1)))
    q_scale = float((QK_NOPE + QK_ROPE) ** -0.5 * math.log2(math.e))

    pend_l = None
    pend_c = None
    for i in range(depth):
        last = i == depth - 1
        mod = ada_linear(ct, ada_w, ada_b, i, rows=b + 1)
        mod = mod.reshape(SUBLANES, ADA_CHUNKS, 1, d)

        fourier = i % 2 == 0
        if pend_l is None:
            hl = norm_block(xl, norm1_g[i], sel_lat, mod=(mod, (0, 1)), pair_rows=fourier)
            hc = norm_block(xc, norm1_g[i], sel_ctx, mod=(mod, (0, 1)), pair_rows=fourier)
        else:
            xl, hl = norm_block(xl, norm1_g[i], sel_lat, res=pend_l, mod=(mod, (0, 1)), write_x=True)
            xc, hc = norm_block(xc, norm1_g[i], sel_ctx, res=pend_c, mod=(mod, (0, 1)), write_x=True)
            if fourier:
                hl, hc = (h.reshape(h.shape[0] // 2, 2 * d) for h in (hl, hc))
        j = i // 2
        if fourier:
            wo = cast_layer(fnet_wo, j)
            fl = pos_dft(me_l, mo_l, chan_dft(hl, wcs), b)
            yl = matmul(fl, wo)
            if not last:
                fc = pos_dft(me_c, mo_c, chan_dft(hc, wcs), b)
                yc = matmul(fc, wo)
        else:
            w_in = mla_w_in[j]
            w_in = jnp.concatenate([w_in, _rot_partner(w_in[:, Q_LORA + KV_LORA:])], axis=1).astype(BF16)
            wq3 = mla_w_uq[j].reshape(Q_LORA, MLA_HEADS, QK_NOPE + QK_ROPE)
            wq_ext = jnp.concatenate([wq3, _rot_partner(wq3[..., QK_NOPE:])], axis=-1)
            wq_ext = wq_ext.reshape(Q_LORA, MLA_HEADS * (QK_NOPE + 2 * QK_ROPE)).astype(BF16)
            w_kv = cast_layer(mla_w_ukv, j)
            wo = cast_layer(mla_wo, j)
            zl = matmul(hl, w_in, out_dtype=F32, bn=w_in.shape[1])
            zc = matmul(hc, w_in, out_dtype=F32, bn=w_in.shape[1])
            q = q_proj(zl, mla_q_g[j] * q_scale, wq_ext, cs_lat)
            kv_l, kpe_l = kv_proj(zl, mla_kv_g[j], w_kv, cs_lat)
            kv_c, kpe_c = kv_proj(zc, mla_kv_g[j], w_kv, cs_ctx)
            o = mla_attention(q, kv_c, kpe_c, kv_l, kpe_l, batch=b, heads=MLA_HEADS)
            yl = matmul(o, wo)
            if not last:
                raise NotImplementedError("context output of an MLA layer that is not last")
        wq = cast_layer(peer_wq, i)
        kk = jnp.stack([peer_k1[i], peer_k2[i]], axis=1).reshape(2 * PEER_HEADS, N_KEYS, PEER_HALF)
        u = cast_layer(peer_u, i)
        v = cast_layer(peer_v, i)
        xl, h2l = norm_block(xl, norm2_g[i], sel_lat, res=(yl, mod, 2), mod=(mod, (3, 4)), write_x=True)
        pend_l = (peer(h2l, wq, kk, u, v), mod, 5)
        if not last:
            xc, h2c = norm_block(xc, norm2_g[i], sel_ctx, res=(yc, mod, 2), mod=(mod, (3, 4)), write_x=True)
            pend_c = (peer(h2c, wq, kk, u, v), mod, 5)

    out = norm_block(xl, final_g, sel_lat, res=pend_l, out_dtype=F32)
    return out.reshape(b, n, d)
```

```python
import functools
import math

import numpy as np
import jax
import jax.numpy as jnp
from jax import lax
from jax.experimental import pallas as pl
from jax.experimental.pallas import tpu as pltpu

F32 = jnp.float32
BF16 = jnp.bfloat16

EPS = 1e-6
ADA_CHUNKS = 6
F_GROUPS = 8
GRID_W = 64
MLA_HEADS = 32
QK_NOPE = 128
QK_ROPE = 64
V_DIM = 128
Q_LORA = 1024
KV_LORA = 512
ROPE_THETA = 10000.0
PEER_HEADS = 8
PEER_HALF = 128
N_KEYS = 128
PEER_TOPK = 16

LANES = 128
SUBLANES = 8
VMEM_LIMIT = 56 * 1024 * 1024
NEG_INF = float("-inf")
NO_RANK = 255.0
MAX_EXCESS = 64.0


def _params(*sem):
    return pltpu.CompilerParams(dimension_semantics=sem, vmem_limit_bytes=VMEM_LIMIT)


def _ada_kernel(ct_ref, w_ref, b_ref, o_ref, sb_ref, *, kc, rows):
    d, bn = w_ref.shape
    nt = bn // LANES

    @pl.when(pl.program_id(0) == 0)
    def _():
        c = ct_ref[...]
        s = c * jax.nn.sigmoid(c)
        for r in range(rows):
            sb_ref[r] = jnp.broadcast_to(s[:, r:r + 1], (d, LANES))

    def body(i, accs):
        k0 = pl.multiple_of(i * kc, kc)
        out = list(accs)
        for t in range(nt):
            wv = w_ref[pl.ds(k0, kc), t * LANES:(t + 1) * LANES]
            for r in range(rows):
                p = wv * sb_ref[r, pl.ds(k0, kc), :]
                out[r * nt + t] = out[r * nt + t] + p.reshape(kc // SUBLANES, SUBLANES, LANES).sum(axis=0)
        return tuple(out)

    init = tuple(jnp.zeros((SUBLANES, LANES), F32) for _ in range(rows * nt))
    accs = lax.fori_loop(0, d // kc, body, init)
    o_ref[...] = jnp.zeros_like(o_ref)
    for r in range(rows):
        for t in range(nt):
            row = accs[r * nt + t].sum(axis=0, keepdims=True) + b_ref[:, t * LANES:(t + 1) * LANES]
            o_ref[r:r + 1, t * LANES:(t + 1) * LANES] = row


def ada_linear(ct, w, b, layer, *, rows, bn=512, kc=256):
    nl, d, n = w.shape
    bn = min(bn, n)
    kc = min(kc, d)
    return pl.pallas_call(
        functools.partial(_ada_kernel, kc=kc, rows=rows),
        grid=(n // bn,),
        in_specs=[
            pl.BlockSpec((d, SUBLANES), lambda j: (0, 0)),
            pl.BlockSpec((None, d, bn), lambda j: (layer, 0, j)),
            pl.BlockSpec((None, 1, bn), lambda j: (layer, 0, j)),
        ],
        out_specs=pl.BlockSpec((SUBLANES, bn), lambda j: (0, j)),
        out_shape=jax.ShapeDtypeStruct((SUBLANES, n), F32),
        scratch_shapes=[pltpu.VMEM((rows, d, LANES), F32)],
        compiler_params=_params("arbitrary"),
        name="ada_linear",
    )(ct, w, b.reshape(nl, 1, n))


def _cast_kernel(w_ref, o_ref):
    o_ref[...] = w_ref[...].astype(o_ref.dtype)


def cast_layer(w, layer, *, bm=512):
    _, r, c = w.shape
    bm = min(bm, r)
    return pl.pallas_call(
        _cast_kernel,
        grid=(r // bm,),
        in_specs=[pl.BlockSpec((None, bm, c), lambda i: (layer, i, 0))],
        out_specs=pl.BlockSpec((bm, c), lambda i: (i, 0)),
        out_shape=jax.ShapeDtypeStruct((r, c), BF16),
        compiler_params=_params("parallel"),
        name="cast_layer",
    )(w)


def _norm_kernel(*refs, has_res, has_mod, write_x, pair_rows):
    it = iter(refs)
    x_ref = next(it)
    if has_res:
        y_ref, gate_ref = next(it), next(it)
    g_ref = next(it)
    if has_mod:
        shift_ref, scale_ref = next(it), next(it)
    if write_x:
        xo_ref = next(it)
    h_ref = next(it)

    def normed(x):
        y = x * lax.rsqrt(jnp.mean(x * x, axis=-1, keepdims=True) + EPS)
        y = y * g_ref[...]
        if has_mod:
            y = y * (1.0 + scale_ref[...]) + shift_ref[...]
        return y.astype(h_ref.dtype)

    if pair_rows:
        bm, d = x_ref.shape
        h = normed(x_ref[...])
        r = lax.broadcasted_iota(jnp.int32, (bm // 2, bm), 0)
        c = lax.broadcasted_iota(jnp.int32, (bm // 2, bm), 1)
        for p in range(2):
            pick = jnp.where(c == 2 * r + p, 1.0, 0.0).astype(h.dtype)
            h_ref[:, p * d:(p + 1) * d] = jnp.dot(pick, h, preferred_element_type=F32).astype(h_ref.dtype)
        return

    x = x_ref[...]
    if has_res:
        x = x + gate_ref[...] * y_ref[...].astype(F32)
    if write_x:
        xo_ref[...] = x
    h_ref[...] = normed(x)


def norm_block(x, g, sel, *, res=None, mod=None, write_x=False, pair_rows=False, out_dtype=BF16, bm=256):
    t, d = x.shape
    bm = min(bm, t)
    assert not (pair_rows and (res is not None or write_x))
    row = pl.BlockSpec((bm, d), lambda i: (i, 0))
    h_shape, h_block = ((t // 2, 2 * d), (bm // 2, 2 * d)) if pair_rows else ((t, d), (bm, d))

    def mod_spec(a):
        return pl.BlockSpec((None, None, 1, d), lambda i, a=a: (sel(i, bm), a, 0, 0))

    args, specs = [x], [row]
    if res is not None:
        args += [res[0], res[1]]
        specs += [row, mod_spec(res[2])]
    args.append(g.reshape(1, d))
    specs.append(pl.BlockSpec((1, d), lambda i: (0, 0)))
    if mod is not None:
        args += [mod[0], mod[0]]
        specs += [mod_spec(mod[1][0]), mod_spec(mod[1][1])]
    out_shape, out_specs = [], []
    if write_x:
        out_shape.append(jax.ShapeDtypeStruct((t, d), F32))
        out_specs.append(row)
    out_shape.append(jax.ShapeDtypeStruct(h_shape, out_dtype))
    out_specs.append(pl.BlockSpec(h_block, lambda i: (i, 0)))
    out = pl.pallas_call(
        functools.partial(_norm_kernel, has_res=res is not None, has_mod=mod is not None, write_x=write_x,
                          pair_rows=pair_rows),
        grid=(t // bm,),
        in_specs=specs,
        out_specs=out_specs,
        out_shape=out_shape,
        compiler_params=_params("parallel"),
        name="norm_block",
    )(*args)
    return out if write_x else out[0]


def _side_specs(side, n_steps, step_of):
    arrays, ins, outs, shapes = [], [], [], []
    for w, layer in side:
        _, r, c = w.shape
        rows = r // n_steps
        assert rows * n_steps == r and rows % (2 * SUBLANES) == 0
        arrays.append(w)
        ins.append(pl.BlockSpec((None, rows, c), lambda *g, layer=layer: (layer, step_of(*g), 0)))
        outs.append(pl.BlockSpec((rows, c), lambda *g: (step_of(*g), 0)))
        shapes.append(jax.ShapeDtypeStruct((r, c), BF16))
    return arrays, ins, outs, shapes


def _with_side_casts(body, n_in, n_side):
    def kernel(*refs):
        main_in, side_in = refs[:n_in], refs[n_in:n_in + n_side]
        rest = refs[n_in + n_side:]
        main_out, side_out = rest[:len(rest) - n_side], rest[len(rest) - n_side:]
        body(*main_in, *main_out)
        for src, dst in zip(side_in, side_out):
            dst[...] = src[...].astype(dst.dtype)
    return kernel


def _mm_kernel(a_ref, b_ref, o_ref):
    o_ref[...] = jnp.dot(a_ref[...], b_ref[...], preferred_element_type=F32).astype(o_ref.dtype)


def matmul(a, b, *, out_dtype=BF16, bm=512, bn=1024, side=()):
    m, k = a.shape
    _, n = b.shape
    bm, bn = min(bm, m), min(bn, n)
    ni = m // bm
    s_arr, s_in, s_out, s_shape = _side_specs(side, (n // bn) * ni, lambda j, i: j * ni + i)
    out = pl.pallas_call(
        _with_side_casts(_mm_kernel, 2, len(side)),
        grid=(n // bn, ni),
        in_specs=[pl.BlockSpec((bm, k), lambda j, i: (i, 0)),
                  pl.BlockSpec((k, bn), lambda j, i: (0, j))] + s_in,
        out_specs=[pl.BlockSpec((bm, bn), lambda j, i: (i, j))] + s_out,
        out_shape=[jax.ShapeDtypeStruct((m, n), out_dtype)] + s_shape,
        compiler_params=_params("parallel", "parallel"),
        name="matmul",
    )(a, b, *s_arr)
    return tuple(out) if side else out[0]


def _dft_mats(n):
    k = np.arange(n, dtype=np.int64)
    ang = 2.0 * np.pi * ((k[:, None] * k[None, :]) % n).astype(np.float64) / n
    s = 1.0 / math.sqrt(n)
    return (np.cos(ang) * s).astype(np.float32), (np.sin(ang) * s).astype(np.float32)


def _chan_dft_kernel(h_ref, w_ref, o_ref):
    gd = h_ref.shape[1]
    y = jnp.dot(h_ref[...], w_ref[...], preferred_element_type=F32)
    o_ref[0] = y[:, :gd].astype(o_ref.dtype)
    o_ref[1] = y[:, gd:].astype(o_ref.dtype)


def chan_dft(h, wcs, *, bm=1024):
    t, d = h.shape
    gd = wcs.shape[0]
    bm = min(bm, t)
    return pl.pallas_call(
        _chan_dft_kernel,
        grid=(t // bm, d // gd),
        in_specs=[pl.BlockSpec((bm, gd), lambda i, g: (i, g)),
                  pl.BlockSpec((gd, 2 * gd), lambda i, g: (0, 0))],
        out_specs=pl.BlockSpec((2, bm, gd), lambda i, g: (0, i, g)),
        out_shape=jax.ShapeDtypeStruct((2, t, d), BF16),
        compiler_params=_params("parallel", "parallel"),
        name="chan_dft",
    )(h, wcs)


def _pos_dft_mats(n):
    cn, sn = _dft_mats(n)
    half = n // 2
    me = np.concatenate([cn[:half, 0::2], -sn[:half, 0::2]], axis=1)
    mo = np.concatenate([cn[:half, 1::2], -sn[:half, 1::2]], axis=1)
    return me, mo


def _pos_dft_kernel(me_ref, mo_ref, yce_ref, yse_ref, yco_ref, yso_ref, o_ref):
    ye = jnp.concatenate([yce_ref[...], yse_ref[...]], axis=0)
    yo = jnp.concatenate([yco_ref[...], yso_ref[...]], axis=0)
    e = jnp.dot(me_ref[...], ye, preferred_element_type=F32)
    o = jnp.dot(mo_ref[...], yo, preferred_element_type=F32)
    o_ref[0] = (e + o).astype(o_ref.dtype)
    o_ref[1] = (e - o).astype(o_ref.dtype)


def pos_dft(me, mo, y2, n_seq, *, bm=512, bn=512):
    half, n = me.shape
    _, t2, d2 = y2.shape
    t, d = 2 * t2, d2 // 2
    bm, bn = min(bm, half), min(bn, d)
    nb = d // bn
    out = pl.pallas_call(
        _pos_dft_kernel,
        grid=(n_seq, nb, half // bm),
        in_specs=[pl.BlockSpec((bm, n), lambda b, j, i: (i, 0)),
                  pl.BlockSpec((bm, n), lambda b, j, i: (i, 0)),
                  pl.BlockSpec((None, half, bn), lambda b, j, i: (0, b, j)),
                  pl.BlockSpec((None, half, bn), lambda b, j, i: (1, b, j)),
                  pl.BlockSpec((None, half, bn), lambda b, j, i: (0, b, nb + j)),
                  pl.BlockSpec((None, half, bn), lambda b, j, i: (1, b, nb + j))],
        out_specs=pl.BlockSpec((None, 2, bm, bn), lambda b, j, i: (b, 0, i, j)),
        out_shape=jax.ShapeDtypeStruct((n_seq, 2, half, d), BF16),
        compiler_params=_params("parallel", "parallel", "parallel"),
        name="pos_dft",
    )(me, mo, y2, y2, y2, y2)
    return out.reshape(t, d)


def _rms_prologue(z, g):
    zn = z * lax.rsqrt(jnp.mean(z * z, axis=-1, keepdims=True) + EPS)
    return (zn * g).astype(BF16)


def _q_proj_kernel(z_ref, g_ref, w_ref, cs_ref, o_ref, zn_ref):
    @pl.when(pl.program_id(1) == 0)
    def _():
        zn_ref[...] = _rms_prologue(z_ref[:, :Q_LORA], g_ref[...])

    acc = jnp.dot(zn_ref[...], w_ref[...], preferred_element_type=F32)
    cs = cs_ref[...]
    hw = QK_NOPE + 2 * QK_ROPE
    for h in range(w_ref.shape[1] // hw):
        o_ref[:, h * hw:h * hw + QK_NOPE] = acc[:, h * hw:h * hw + QK_NOPE].astype(o_ref.dtype)
        t = acc[:, h * hw + QK_NOPE:(h + 1) * hw] * cs
        r = t + pltpu.roll(t, QK_ROPE, axis=1)
        o_ref[:, h * hw + QK_NOPE:(h + 1) * hw] = r.astype(o_ref.dtype)


def q_proj(z, q_g, w_q, cs_tab, *, bm=1024, bn=2048):
    t, zw = z.shape
    n = w_q.shape[1]
    bm = min(bm, t)
    return pl.pallas_call(
        _q_proj_kernel,
        grid=(t // bm, n // bn),
        in_specs=[pl.BlockSpec((bm, zw), lambda i, j: (i, 0)),
                  pl.BlockSpec((1, Q_LORA), lambda i, j: (0, 0)),
                  pl.BlockSpec((Q_LORA, bn), lambda i, j: (0, j)),
                  pl.BlockSpec((bm, 2 * QK_ROPE), lambda i, j: (i, 0))],
        out_specs=pl.BlockSpec((bm, bn), lambda i, j: (i, j)),
        out_shape=jax.ShapeDtypeStruct((t, n), BF16),
        scratch_shapes=[pltpu.VMEM((bm, Q_LORA), BF16)],
        compiler_params=_params("parallel", "arbitrary"),
        name="q_proj",
    )(z, q_g.reshape(1, Q_LORA), w_q, cs_tab)


def _kv_proj_kernel(z_ref, g_ref, w_ref, cs_ref, o_ref, kpe_ref, zn_ref):
    @pl.when(pl.program_id(1) == 0)
    def _():
        zn_ref[...] = _rms_prologue(z_ref[:, Q_LORA:Q_LORA + KV_LORA], g_ref[...])
        t = z_ref[:, Q_LORA + KV_LORA:] * cs_ref[...]
        r = t + pltpu.roll(t, QK_ROPE, axis=1)
        lane = lax.broadcasted_iota(jnp.int32, r.shape, 1)
        kpe_ref[...] = jnp.where(lane < QK_ROPE, r, 0.0).astype(kpe_ref.dtype)

    o_ref[...] = jnp.dot(zn_ref[...], w_ref[...], preferred_element_type=F32).astype(o_ref.dtype)


def kv_proj(z, kv_g, w_kv, cs_tab, *, bm=1024, bn=2048):
    t, zw = z.shape
    n = w_kv.shape[1]
    bm = min(bm, t)
    return pl.pallas_call(
        _kv_proj_kernel,
        grid=(t // bm, n // bn),
        in_specs=[pl.BlockSpec((bm, zw), lambda i, j: (i, 0)),
                  pl.BlockSpec((1, KV_LORA), lambda i, j: (0, 0)),
                  pl.BlockSpec((KV_LORA, bn), lambda i, j: (0, j)),
                  pl.BlockSpec((bm, 2 * QK_ROPE), lambda i, j: (i, 0))],
        out_specs=[pl.BlockSpec((bm, bn), lambda i, j: (i, j)),
                   pl.BlockSpec((bm, 2 * QK_ROPE), lambda i, j: (i, 0))],
        out_shape=[jax.ShapeDtypeStruct((t, n), BF16),
                   jax.ShapeDtypeStruct((t, 2 * QK_ROPE), BF16)],
        scratch_shapes=[pltpu.VMEM((bm, KV_LORA), BF16)],
        compiler_params=_params("parallel", "arbitrary"),
        name="kv_proj",
    )(z, kv_g.reshape(1, KV_LORA), w_kv, cs_tab)


def _attn_kernel(q_ref, kc_ref, pc_ref, vc_ref, kl_ref, pl_ref, vl_ref, o_ref, *, tq, tk):
    n = q_ref.shape[0]
    n_lat = kl_ref.shape[0]
    nt = (((1,), (1,)), ((), ()))

    def key_chunks():
        yield (jnp.concatenate([kc_ref[...], pc_ref[...]], axis=1), vc_ref[...])
        for c in range(n_lat // tk):
            rows = slice(c * tk, (c + 1) * tk)
            yield (jnp.concatenate([kl_ref[rows, :], pl_ref[rows, :]], axis=1), vl_ref[rows, :])

    def lane_max(s):
        m = s[:, :LANES]
        for j in range(1, s.shape[1] // LANES):
            m = jnp.maximum(m, s[:, j * LANES:(j + 1) * LANES])
        return m

    def one_pass(q):
        ref = None
        acc = None
        top = None
        for k, v in key_chunks():
            s = lax.dot_general(q, k, nt, preferred_element_type=F32)
            if ref is None:
                ref = jnp.max(s, axis=-1, keepdims=True)
            p = jnp.exp2(s - ref).astype(BF16)
            v1 = jnp.concatenate([v, jnp.ones_like(v)], axis=1)
            part = jnp.dot(p, v1, preferred_element_type=F32)
            acc = part if acc is None else acc + part
            m = lane_max(s)
            top = m if top is None else jnp.maximum(top, m)
        return acc[:, :V_DIM] / acc[:, V_DIM:], jnp.max(top - ref)

    def online(q):
        m = jnp.full((tq, 1), NEG_INF, F32)
        l = jnp.zeros((tq, 1), F32)
        acc = jnp.zeros((tq, V_DIM), F32)
        for k, v in key_chunks():
            s = lax.dot_general(q, k, nt, preferred_element_type=F32)
            m_new = jnp.maximum(m, jnp.max(s, axis=-1, keepdims=True))
            alpha = jnp.exp2(m - m_new)
            p = jnp.exp2(s - m_new)
            l = alpha * l + jnp.sum(p, axis=-1, keepdims=True)
            acc = alpha * acc + jnp.dot(p.astype(BF16), v, preferred_element_type=F32)
            m = m_new
        return acc / l

    def q_tile(i, carry):
        r0 = pl.multiple_of(i * tq, tq)
        q = q_ref[pl.ds(r0, tq), :]
        out, excess = one_pass(q)
        o_ref[pl.ds(r0, tq), :] = out.astype(o_ref.dtype)

        @pl.when(jnp.logical_not(excess <= MAX_EXCESS))
        def _():
            o_ref[pl.ds(r0, tq), :] = online(q).astype(o_ref.dtype)

        return carry

    lax.fori_loop(0, n // tq, q_tile, 0)


def mla_attention(q, kv_c, kpe_c, kv_l, kpe_l, *, batch, heads, tq=512, tk=512):
    n = q.shape[0] // batch
    n_ctx = kv_c.shape[0] // batch
    tq, tk = min(tq, n), min(tk, n)
    hq = QK_NOPE + 2 * QK_ROPE
    return pl.pallas_call(
        functools.partial(_attn_kernel, tq=tq, tk=tk),
        grid=(batch, heads),
        in_specs=[pl.BlockSpec((n, hq), lambda b, h: (b, h)),
                  pl.BlockSpec((n_ctx, QK_NOPE), lambda b, h: (b, 2 * h)),
                  pl.BlockSpec((n_ctx, 2 * QK_ROPE), lambda b, h: (b, 0)),
                  pl.BlockSpec((n_ctx, V_DIM), lambda b, h: (b, 2 * h + 1)),
                  pl.BlockSpec((n, QK_NOPE), lambda b, h: (b, 2 * h)),
                  pl.BlockSpec((n, 2 * QK_ROPE), lambda b, h: (b, 0)),
                  pl.BlockSpec((n, V_DIM), lambda b, h: (b, 2 * h + 1))],
        out_specs=pl.BlockSpec((n, V_DIM), lambda b, h: (b, h)),
        out_shape=jax.ShapeDtypeStruct((batch * n, heads * V_DIM), BF16),
        compiler_params=_params("parallel", "parallel"),
        name="mla_attention",
    )(q, kv_c, kpe_c, kv_c, kv_l, kpe_l, kv_l)


def _peer_scores_kernel(h_ref, wq_ref, kk_ref, o_ref):
    q = jnp.dot(h_ref[...], wq_ref[...], preferred_element_type=F32)
    nt = (((1,), (1,)), ((), ()))
    for c in range(wq_ref.shape[1] // PEER_HALF):
        qc = q[:, c * PEER_HALF:(c + 1) * PEER_HALF]
        o_ref[c] = lax.dot_general(kk_ref[c], qc, nt, precision=lax.Precision.HIGHEST,
                                   preferred_element_type=F32)


def peer_scores(h, wq, kk, *, bm=512, bn=1024):
    t, d = h.shape
    n = wq.shape[1]
    bm, bn = min(bm, t), min(bn, n)
    cb = bn // PEER_HALF
    return pl.pallas_call(
        _peer_scores_kernel,
        grid=(n // bn, t // bm),
        in_specs=[pl.BlockSpec((bm, d), lambda j, i: (i, 0)),
                  pl.BlockSpec((d, bn), lambda j, i: (0, j)),
                  pl.BlockSpec((cb, N_KEYS, PEER_HALF), lambda j, i: (j, 0, 0))],
        out_specs=pl.BlockSpec((cb, N_KEYS, bm), lambda j, i: (j, 0, i)),
        out_shape=jax.ShapeDtypeStruct((n // PEER_HALF, N_KEYS, t), F32),
        compiler_params=_params("parallel", "parallel"),
        name="peer_scores",
    )(h, wq, kk)


_HALF_K = PEER_TOPK // 2


def _top16(s, iota):
    vals = []
    rank = jnp.full(s.shape, int(NO_RANK), jnp.int32)
    for r in range(PEER_TOPK):
        m = jnp.max(s, axis=0, keepdims=True)
        idx = jnp.min(jnp.where(s == m, iota, N_KEYS), axis=0, keepdims=True)
        hit = iota == idx
        rank = jnp.where(hit, r, rank)
        s = jnp.where(hit, NEG_INF, s)
        vals.append(m)
    return vals, rank


def _extract16(s, with_rank=False):
    vals = []
    rank = jnp.full(s.shape, NO_RANK, F32) if with_rank else None
    for r in range(PEER_TOPK):
        m = jnp.max(s, axis=0, keepdims=True)
        vals.append(m)
        hit = s == m
        if with_rank:
            rank = jnp.where(hit, float(r), rank)
        s = jnp.where(hit, NEG_INF, s)
    removed = jnp.sum(jnp.where(s == NEG_INF, 1.0, 0.0), axis=0, keepdims=True)
    return vals, removed != float(PEER_TOPK), rank


def _candidates(a, b):
    a_hi = jnp.concatenate(a[_HALF_K:], axis=0)
    b_all = jnp.concatenate(b, axis=0)
    b_lo = b_all[:_HALF_K]
    return jnp.concatenate([a[0] + b_all] + [a[r] + b_lo for r in range(1, _HALF_K)]
                           + [a_hi + b[0]], axis=0)


def _peer_topk_fast(s_ref, k1_ref, k2_ref, h):
    tb = s_ref.shape[2]
    s1 = s_ref[2 * h]
    s2 = s_ref[2 * h + 1]
    a, bad1, _ = _extract16(s1)
    b, bad2, rank2 = _extract16(s2, with_rank=True)
    cand = _candidates(a, b)
    sums, bad3, _ = _extract16(cand)
    keep = jnp.where(cand >= sums[-1], 1.0, 0.0)
    z = jnp.sum(keep * jnp.exp(cand - sums[0]), axis=0, keepdims=True)
    lim = jnp.zeros((N_KEYS, tb), F32)
    for r, count in enumerate(_rank1_counts(keep)):
        lim = jnp.where(s1 == a[r], count, lim)
    _emit(k1_ref, k2_ref, h, lim, rank2, jnp.exp(s1 - a[0]) * (1.0 / z), jnp.exp(s2 - b[0]))
    return bad1 | bad2 | bad3


def _rank1_counts(sel):
    counts = [jnp.sum(sel[:PEER_TOPK], axis=0, keepdims=True)]
    for r in range(1, _HALF_K):
        lo = PEER_TOPK + (r - 1) * _HALF_K
        counts.append(jnp.sum(sel[lo:lo + _HALF_K], axis=0, keepdims=True))
    tail = sel[PEER_TOPK + (_HALF_K - 1) * _HALF_K:]
    return counts + [tail[r:r + 1] for r in range(_HALF_K)]


def _emit(k1_ref, k2_ref, h, count, rank2, p1, p2):
    k1_ref[h, 0] = 1.0 - count
    k1_ref[h, 1] = p1
    k2_ref[h, 0] = (-rank2).astype(k2_ref.dtype)
    k2_ref[h, 1] = p2.astype(k2_ref.dtype)


def _peer_topk_kernel(s_ref, k1_ref, k2_ref):
    for h in range(s_ref.shape[0] // 2):
        bad = _peer_topk_fast(s_ref, k1_ref, k2_ref, h)

        @pl.when(jnp.max(jnp.where(bad, 1.0, 0.0)) > 0.0)
        def _():
            _peer_topk_exact(s_ref, k1_ref, k2_ref, h)


def _peer_topk_exact(s_ref, k1_ref, k2_ref, h):
    tb = s_ref.shape[2]
    iota = lax.broadcasted_iota(jnp.int32, (N_KEYS, tb), 0)
    i16 = lax.broadcasted_iota(jnp.int32, (PEER_TOPK, tb), 0)
    i8 = lax.broadcasted_iota(jnp.int32, (_HALF_K, tb), 0)
    flat = jnp.concatenate([i16] + [i8 + PEER_TOPK * r for r in range(1, _HALF_K)]
                           + [(i8 + _HALF_K) * PEER_TOPK], axis=0)
    s1 = s_ref[2 * h]
    s2 = s_ref[2 * h + 1]
    a, rank1 = _top16(s1, iota)
    b, rank2 = _top16(s2, iota)
    cand = _candidates(a, b)
    cmax = a[0] + b[0]
    sel = jnp.zeros(flat.shape, jnp.bool_)
    c = cand
    for _ in range(PEER_TOPK):
        m = jnp.max(c, axis=0, keepdims=True)
        idx = jnp.min(jnp.where(c == m, flat, PEER_TOPK * PEER_TOPK), axis=0, keepdims=True)
        hit = flat == idx
        sel = jnp.logical_or(sel, hit)
        c = jnp.where(hit, NEG_INF, c)
    self = sel.astype(F32)
    z = jnp.sum(self * jnp.exp(cand - cmax), axis=0, keepdims=True)
    lim = jnp.zeros((N_KEYS, tb), F32)
    for r, count in enumerate(_rank1_counts(self)):
        lim = jnp.where(rank1 == r, count, lim)
    _emit(k1_ref, k2_ref, h, lim, rank2.astype(F32), jnp.exp(s1 - a[0]) * (1.0 / z), jnp.exp(s2 - b[0]))


def peer_topk(scores, *, tb=256):
    c2, nk, t = scores.shape
    tb = min(tb, t)
    hh = c2 // 2
    return pl.pallas_call(
        _peer_topk_kernel,
        grid=(t // tb,),
        in_specs=[pl.BlockSpec((c2, nk, tb), lambda i: (0, 0, i))],
        out_specs=[pl.BlockSpec((hh, 2, nk, tb), lambda i: (0, 0, 0, i)),
                   pl.BlockSpec((hh, 2, nk, tb), lambda i: (0, 0, 0, i))],
        out_shape=[jax.ShapeDtypeStruct((hh, 2, nk, t), F32),
                   jax.ShapeDtypeStruct((hh, 2, nk, t), BF16)],
        compiler_params=_params("parallel"),
        name="peer_topk",
    )(scores)


def _gelu(x):
    return 0.5 * x * (1.0 + lax.erf(x * (1.0 / math.sqrt(2.0))))


def _peer_dense_kernel(h_ref, u_ref, v_ref, k1_ref, k2_ref, o_ref):
    j = pl.program_id(1)
    n1 = u_ref.shape[0] // N_KEYS
    tb = h_ref.shape[0]
    nt = (((1,), (1,)), ((), ()))
    tn = (((0,), (0,)), ((), ()))
    odd = (j % 2) == 1

    @pl.when(j == 0)
    def _():
        o_ref[...] = jnp.zeros_like(o_ref)

    a_t = lax.dot_general(u_ref[...], h_ref[...], nt, preferred_element_type=F32)
    parts = []
    for c in range(n1):
        g = jnp.zeros((N_KEYS, tb), BF16)
        for h in range(k1_ref.shape[0]):
            lo, hi = k1_ref[h, :, c:c + 1, :], k1_ref[h, :, n1 + c:n1 + c + 1, :]
            thr = jnp.broadcast_to(jnp.where(odd, hi[0], lo[0]), (N_KEYS, tb)).astype(BF16)
            p1 = jnp.broadcast_to(jnp.where(odd, hi[1], lo[1]), (N_KEYS, tb)).astype(BF16)
            g = g + jnp.where(k2_ref[h, 0] >= thr, k2_ref[h, 1] * p1, jnp.zeros((), BF16))
        parts.append(g)
    gate = jnp.concatenate(parts, axis=0)
    w_t = gate * _gelu(a_t).astype(BF16)
    o_ref[...] += lax.dot_general(w_t, v_ref[...], tn, preferred_element_type=F32)


def peer_dense(h, u, v, key1, key2, *, tb=512, ec=512, side=()):
    t, d = h.shape
    e = u.shape[0]
    tb = min(tb, t)
    hh, _, nk, _ = key1.shape
    assert 2 * (ec // nk) == SUBLANES
    nj = e // ec
    s_arr, s_in, s_out, s_shape = _side_specs(side, (t // tb) * nj, lambda i, j: i * nj + j)
    out = pl.pallas_call(
        _with_side_casts(_peer_dense_kernel, 5, len(side)),
        grid=(t // tb, nj),
        in_specs=[pl.BlockSpec((tb, d), lambda i, j: (i, 0)),
                  pl.BlockSpec((ec, d), lambda i, j: (j, 0)),
                  pl.BlockSpec((ec, d), lambda i, j: (j, 0)),
                  pl.BlockSpec((hh, 2, SUBLANES, tb), lambda i, j: (0, 0, j // 2, i)),
                  pl.BlockSpec((hh, 2, nk, tb), lambda i, j: (0, 0, 0, i))] + s_in,
        out_specs=[pl.BlockSpec((tb, d), lambda i, j: (i, 0))] + s_out,
        out_shape=[jax.ShapeDtypeStruct((t, d), F32)] + s_shape,
        compiler_params=_params("parallel", "arbitrary"),
        name="peer_dense",
    )(h, u, v, key1, key2, *s_arr)
    return tuple(out) if side else out[0]


def peer(h, wq, kk, u, v, side=()):
    scores = peer_scores(h, wq, kk)
    key1, key2 = peer_topk(scores)
    return peer_dense(h, u, v, key1, key2, side=side)


def _rope_tables(n):
    rows = n // GRID_W
    row_id = np.repeat(np.arange(rows), GRID_W).astype(np.float64)
    col_id = np.tile(np.arange(GRID_W), rows).astype(np.float64)
    axis_dim = QK_ROPE // 2
    inv_freq = ROPE_THETA ** (-np.arange(0, axis_dim, 2, dtype=np.float64) / axis_dim)
    ang = np.stack([row_id[:, None] * inv_freq, col_id[:, None] * inv_freq], axis=1)
    cos = np.cos(ang)
    sin = np.sin(ang)
    cos_full = np.concatenate([cos, cos], axis=-1).reshape(n, QK_ROPE)
    sin_full = np.concatenate([sin, sin], axis=-1).reshape(n, QK_ROPE)
    return np.concatenate([cos_full, sin_full], axis=-1).astype(np.float32)


def _rot_partner(w):
    ws = w.reshape(w.shape[:-1] + (2, 2, QK_ROPE // 4))
    return jnp.stack([-ws[..., 1, :], ws[..., 0, :]], axis=-2).reshape(w.shape)


def kernel(x, c, ctx, c_ctx, ada_w, ada_b, norm1_g, norm2_g, final_g, fnet_wo,
           mla_w_in, mla_q_g, mla_w_uq, mla_kv_g, mla_w_ukv, mla_wo,
           peer_wq, peer_k1, peer_k2, peer_u, peer_v):
    b, n, d = x.shape
    n_ctx = ctx.shape[1]
    depth = ada_w.shape[0]
    gd = d // F_GROUPS

    xl = x.reshape(b * n, d)
    xc = ctx.reshape(b * n_ctx, d)

    ct = jnp.zeros((d, SUBLANES), F32).at[:, :b].set(c.T).at[:, b].set(c_ctx)

    def sel_lat(i, bm):
        return (i * bm) // n

    def sel_ctx(i, bm):
        return b

    cc, sc = _dft_mats(gd)
    wcs = jnp.asarray(np.concatenate([cc, sc], axis=1)).astype(BF16)
    me_l, mo_l = (jnp.asarray(m).astype(BF16) for m in _pos_dft_mats(n))
    me_c, mo_c = (jnp.asarray(m).astype(BF16) for m in _pos_dft_mats(n_ctx))
    cs_lat = jnp.asarray(np.tile(_rope_tables(n), (b, 1)))
    cs_ctx = jnp.asarray(np.tile(np.concatenate([np.ones((1, QK_ROPE), np.float32),
                                                 np.zeros((1, QK_ROPE), np.float32)], axis=1),
                                 (b * n_ctx, 1)))
    q_scale = float((QK_NOPE + QK_ROPE) ** -0.5 * math.log2(math.e))

    pend_l = None
    pend_c = None
    uv = None
    for i in range(depth):
        last = i == depth - 1
        mod = ada_linear(ct, ada_w, ada_b, i, rows=b + 1)
        mod = mod.reshape(SUBLANES, ADA_CHUNKS, 1, d)

        fourier = i % 2 == 0
        if pend_l is None:
            hl = norm_block(xl, norm1_g[i], sel_lat, mod=(mod, (0, 1)), pair_rows=fourier)
            hc = norm_block(xc, norm1_g[i], sel_ctx, mod=(mod, (0, 1)), pair_rows=fourier)
        else:
            xl, hl = norm_block(xl, norm1_g[i], sel_lat, res=pend_l, mod=(mod, (0, 1)), write_x=True)
            xc, hc = norm_block(xc, norm1_g[i], sel_ctx, res=pend_c, mod=(mod, (0, 1)), write_x=True)
            if fourier:
                hl, hc = (h.reshape(h.shape[0] // 2, 2 * d) for h in (hl, hc))
        j = i // 2
        if fourier:
            wo = cast_layer(fnet_wo, j)
            fl = pos_dft(me_l, mo_l, chan_dft(hl, wcs), b)
            if uv is None:
                yl, *uv = matmul(fl, wo, side=[(peer_u, i), (peer_v, i)])
            else:
                yl = matmul(fl, wo)
            if not last:
                fc = pos_dft(me_c, mo_c, chan_dft(hc, wcs), b)
                yc = matmul(fc, wo)
        else:
            w_in = mla_w_in[j]
            w_in = jnp.concatenate([w_in, _rot_partner(w_in[:, Q_LORA + KV_LORA:])], axis=1).astype(BF16)
            wq3 = mla_w_uq[j].reshape(Q_LORA, MLA_HEADS, QK_NOPE + QK_ROPE)
            wq_ext = jnp.concatenate([wq3, _rot_partner(wq3[..., QK_NOPE:])], axis=-1)
            wq_ext = wq_ext.reshape(Q_LORA, MLA_HEADS * (QK_NOPE + 2 * QK_ROPE)).astype(BF16)
            w_kv = cast_layer(mla_w_ukv, j)
            wo = cast_layer(mla_wo, j)
            zl = matmul(hl, w_in, out_dtype=F32, bn=w_in.shape[1])
            zc = matmul(hc, w_in, out_dtype=F32, bn=w_in.shape[1])
            q = q_proj(zl, mla_q_g[j] * q_scale, wq_ext, cs_lat)
            kv_l, kpe_l = kv_proj(zl, mla_kv_g[j], w_kv, cs_lat)
            kv_c, kpe_c = kv_proj(zc, mla_kv_g[j], w_kv, cs_ctx)
            o = mla_attention(q, kv_c, kpe_c, kv_l, kpe_l, batch=b, heads=MLA_HEADS)
            yl = matmul(o, wo)
            if not last:
                raise NotImplementedError("context output of an MLA layer that is not last")
        wq = cast_layer(peer_wq, i)
        kk = jnp.stack([peer_k1[i], peer_k2[i]], axis=1).reshape(2 * PEER_HEADS, N_KEYS, PEER_HALF)
        u, v = uv if uv is not None else (cast_layer(peer_u, i), cast_layer(peer_v, i))
        xl, h2l = norm_block(xl, norm2_g[i], sel_lat, res=(yl, mod, 2), mod=(mod, (3, 4)), write_x=True)
        if last:
            fl = peer(h2l, wq, kk, u, v)
        else:
            fl, *uv = peer(h2l, wq, kk, u, v, side=[(peer_u, i + 1), (peer_v, i + 1)])
        pend_l = (fl, mod, 5)
        if not last:
            xc, h2c = norm_block(xc, norm2_g[i], sel_ctx, res=(yc, mod, 2), mod=(mod, (3, 4)), write_x=True)
            pend_c = (peer(h2c, wq, kk, u, v), mod, 5)

    out = norm_block(xl, final_g, sel_lat, res=pend_l, out_dtype=F32)
    return out.reshape(b, n, d)
```

```python
import functools
import math

import numpy as np
import jax
import jax.numpy as jnp
from jax import lax
from jax.experimental import pallas as pl
from jax.experimental.pallas import tpu as pltpu

F32 = jnp.float32
BF16 = jnp.bfloat16

EPS = 1e-6
ADA_CHUNKS = 6
F_GROUPS = 8
GRID_W = 64
MLA_HEADS = 32
QK_NOPE = 128
QK_ROPE = 64
V_DIM = 128
Q_LORA = 1024
KV_LORA = 512
ROPE_THETA = 10000.0
PEER_HEADS = 8
PEER_HALF = 128
N_KEYS = 128
PEER_TOPK = 16

LANES = 128
SUBLANES = 8
VMEM_LIMIT = 56 * 1024 * 1024
NEG_INF = float("-inf")
TOPK_GROUP = 4
NO_RANK = 255.0
MAX_EXCESS = 64.0


def _params(*sem):
    return pltpu.CompilerParams(dimension_semantics=sem, vmem_limit_bytes=VMEM_LIMIT)


def _ada_kernel(ct_ref, w_ref, b_ref, o_ref, sb_ref, *, kc, rows):
    d, bn = w_ref.shape
    nt = bn // LANES

    @pl.when(pl.program_id(0) == 0)
    def _():
        c = ct_ref[...]
        s = c * jax.nn.sigmoid(c)
        for r in range(rows):
            sb_ref[r] = jnp.broadcast_to(s[:, r:r + 1], (d, LANES))

    def body(i, accs):
        k0 = pl.multiple_of(i * kc, kc)
        out = list(accs)
        for t in range(nt):
            wv = w_ref[pl.ds(k0, kc), t * LANES:(t + 1) * LANES]
            for r in range(rows):
                p = wv * sb_ref[r, pl.ds(k0, kc), :]
                out[r * nt + t] = out[r * nt + t] + p.reshape(kc // SUBLANES, SUBLANES, LANES).sum(axis=0)
        return tuple(out)

    init = tuple(jnp.zeros((SUBLANES, LANES), F32) for _ in range(rows * nt))
    accs = lax.fori_loop(0, d // kc, body, init)
    o_ref[...] = jnp.zeros_like(o_ref)
    for r in range(rows):
        for t in range(nt):
            row = accs[r * nt + t].sum(axis=0, keepdims=True) + b_ref[:, t * LANES:(t + 1) * LANES]
            o_ref[r:r + 1, t * LANES:(t + 1) * LANES] = row


def ada_linear(ct, w, b, layer, *, rows, bn=512, kc=256):
    nl, d, n = w.shape
    bn = min(bn, n)
    kc = min(kc, d)
    return pl.pallas_call(
        functools.partial(_ada_kernel, kc=kc, rows=rows),
        grid=(n // bn,),
        in_specs=[
            pl.BlockSpec((d, SUBLANES), lambda j: (0, 0)),
            pl.BlockSpec((None, d, bn), lambda j: (layer, 0, j)),
            pl.BlockSpec((None, 1, bn), lambda j: (layer, 0, j)),
        ],
        out_specs=pl.BlockSpec((SUBLANES, bn), lambda j: (0, j)),
        out_shape=jax.ShapeDtypeStruct((SUBLANES, n), F32),
        scratch_shapes=[pltpu.VMEM((rows, d, LANES), F32)],
        compiler_params=_params("arbitrary"),
        name="ada_linear",
    )(ct, w, b.reshape(nl, 1, n))


def _cast_kernel(w_ref, o_ref):
    o_ref[...] = w_ref[...].astype(o_ref.dtype)


def cast_layer(w, layer, *, bm=512):
    _, r, c = w.shape
    bm = min(bm, r)
    return pl.pallas_call(
        _cast_kernel,
        grid=(r // bm,),
        in_specs=[pl.BlockSpec((None, bm, c), lambda i: (layer, i, 0))],
        out_specs=pl.BlockSpec((bm, c), lambda i: (i, 0)),
        out_shape=jax.ShapeDtypeStruct((r, c), BF16),
        compiler_params=_params("parallel"),
        name="cast_layer",
    )(w)


def _norm_kernel(*refs, has_res, has_mod, write_x, pair_rows):
    it = iter(refs)
    x_ref = next(it)
    if has_res:
        y_ref, gate_ref = next(it), next(it)
    g_ref = next(it)
    if has_mod:
        shift_ref, scale_ref = next(it), next(it)
    if write_x:
        xo_ref = next(it)
    h_ref = next(it)

    def normed(x):
        y = x * lax.rsqrt(jnp.mean(x * x, axis=-1, keepdims=True) + EPS)
        y = y * g_ref[...]
        if has_mod:
            y = y * (1.0 + scale_ref[...]) + shift_ref[...]
        return y.astype(h_ref.dtype)

    if pair_rows:
        bm, d = x_ref.shape
        h = normed(x_ref[...])
        r = lax.broadcasted_iota(jnp.int32, (bm // 2, bm), 0)
        c = lax.broadcasted_iota(jnp.int32, (bm // 2, bm), 1)
        for p in range(2):
            pick = jnp.where(c == 2 * r + p, 1.0, 0.0).astype(h.dtype)
            h_ref[:, p * d:(p + 1) * d] = jnp.dot(pick, h, preferred_element_type=F32).astype(h_ref.dtype)
        return

    x = x_ref[...]
    if has_res:
        x = x + gate_ref[...] * y_ref[...].astype(F32)
    if write_x:
        xo_ref[...] = x
    h_ref[...] = normed(x)


def norm_block(x, g, sel, *, res=None, mod=None, write_x=False, pair_rows=False, out_dtype=BF16, bm=256):
    t, d = x.shape
    bm = min(bm, t)
    assert not (pair_rows and (res is not None or write_x))
    row = pl.BlockSpec((bm, d), lambda i: (i, 0))
    h_shape, h_block = ((t // 2, 2 * d), (bm // 2, 2 * d)) if pair_rows else ((t, d), (bm, d))

    def mod_spec(a):
        return pl.BlockSpec((None, None, 1, d), lambda i, a=a: (sel(i, bm), a, 0, 0))

    args, specs = [x], [row]
    if res is not None:
        args += [res[0], res[1]]
        specs += [row, mod_spec(res[2])]
    args.append(g.reshape(1, d))
    specs.append(pl.BlockSpec((1, d), lambda i: (0, 0)))
    if mod is not None:
        args += [mod[0], mod[0]]
        specs += [mod_spec(mod[1][0]), mod_spec(mod[1][1])]
    out_shape, out_specs = [], []
    if write_x:
        out_shape.append(jax.ShapeDtypeStruct((t, d), F32))
        out_specs.append(row)
    out_shape.append(jax.ShapeDtypeStruct(h_shape, out_dtype))
    out_specs.append(pl.BlockSpec(h_block, lambda i: (i, 0)))
    out = pl.pallas_call(
        functools.partial(_norm_kernel, has_res=res is not None, has_mod=mod is not None, write_x=write_x,
                          pair_rows=pair_rows),
        grid=(t // bm,),
        in_specs=specs,
        out_specs=out_specs,
        out_shape=out_shape,
        compiler_params=_params("parallel"),
        name="norm_block",
    )(*args)
    return out if write_x else out[0]


def _side_specs(side, n_steps, step_of):
    arrays, ins, outs, shapes = [], [], [], []
    for w, layer in side:
        _, r, c = w.shape
        rows = r // n_steps
        assert rows * n_steps == r and rows % (2 * SUBLANES) == 0
        arrays.append(w)
        ins.append(pl.BlockSpec((None, rows, c), lambda *g, layer=layer: (layer, step_of(*g), 0)))
        outs.append(pl.BlockSpec((rows, c), lambda *g: (step_of(*g), 0)))
        shapes.append(jax.ShapeDtypeStruct((r, c), BF16))
    return arrays, ins, outs, shapes


def _with_side_casts(body, n_in, n_side):
    def kernel(*refs):
        main_in, side_in = refs[:n_in], refs[n_in:n_in + n_side]
        rest = refs[n_in + n_side:]
        main_out, side_out = rest[:len(rest) - n_side], rest[len(rest) - n_side:]
        body(*main_in, *main_out)
        for src, dst in zip(side_in, side_out):
            dst[...] = src[...].astype(dst.dtype)
    return kernel


def _mm_kernel(a_ref, b_ref, o_ref):
    o_ref[...] = jnp.dot(a_ref[...], b_ref[...], preferred_element_type=F32).astype(o_ref.dtype)


def matmul(a, b, *, out_dtype=BF16, bm=512, bn=1024, side=()):
    m, k = a.shape
    _, n = b.shape
    bm, bn = min(bm, m), min(bn, n)
    ni = m // bm
    s_arr, s_in, s_out, s_shape = _side_specs(side, (n // bn) * ni, lambda j, i: j * ni + i)
    out = pl.pallas_call(
        _with_side_casts(_mm_kernel, 2, len(side)),
        grid=(n // bn, ni),
        in_specs=[pl.BlockSpec((bm, k), lambda j, i: (i, 0)),
                  pl.BlockSpec((k, bn), lambda j, i: (0, j))] + s_in,
        out_specs=[pl.BlockSpec((bm, bn), lambda j, i: (i, j))] + s_out,
        out_shape=[jax.ShapeDtypeStruct((m, n), out_dtype)] + s_shape,
        compiler_params=_params("parallel", "parallel"),
        name="matmul",
    )(a, b, *s_arr)
    return tuple(out) if side else out[0]


def _dft_mats(n):
    k = np.arange(n, dtype=np.int64)
    ang = 2.0 * np.pi * ((k[:, None] * k[None, :]) % n).astype(np.float64) / n
    s = 1.0 / math.sqrt(n)
    return (np.cos(ang) * s).astype(np.float32), (np.sin(ang) * s).astype(np.float32)


def _chan_dft_kernel(h_ref, w_ref, o_ref):
    gd = h_ref.shape[1]
    y = jnp.dot(h_ref[...], w_ref[...], preferred_element_type=F32)
    o_ref[0] = y[:, :gd].astype(o_ref.dtype)
    o_ref[1] = y[:, gd:].astype(o_ref.dtype)


def chan_dft(h, wcs, *, bm=1024):
    t, d = h.shape
    gd = wcs.shape[0]
    bm = min(bm, t)
    return pl.pallas_call(
        _chan_dft_kernel,
        grid=(t // bm, d // gd),
        in_specs=[pl.BlockSpec((bm, gd), lambda i, g: (i, g)),
                  pl.BlockSpec((gd, 2 * gd), lambda i, g: (0, 0))],
        out_specs=pl.BlockSpec((2, bm, gd), lambda i, g: (0, i, g)),
        out_shape=jax.ShapeDtypeStruct((2, t, d), BF16),
        compiler_params=_params("parallel", "parallel"),
        name="chan_dft",
    )(h, wcs)


def _pos_dft_mats(n):
    cn, sn = _dft_mats(n)
    half = n // 2
    me = np.concatenate([cn[:half, 0::2], -sn[:half, 0::2]], axis=1)
    mo = np.concatenate([cn[:half, 1::2], -sn[:half, 1::2]], axis=1)
    return me, mo


def _pos_dft_kernel(me_ref, mo_ref, yce_ref, yse_ref, yco_ref, yso_ref, o_ref):
    ye = jnp.concatenate([yce_ref[...], yse_ref[...]], axis=0)
    yo = jnp.concatenate([yco_ref[...], yso_ref[...]], axis=0)
    e = jnp.dot(me_ref[...], ye, preferred_element_type=F32)
    o = jnp.dot(mo_ref[...], yo, preferred_element_type=F32)
    o_ref[0] = (e + o).astype(o_ref.dtype)
    o_ref[1] = (e - o).astype(o_ref.dtype)


def pos_dft(me, mo, y2, n_seq, *, bm=512, bn=512, side=()):
    half, n = me.shape
    _, t2, d2 = y2.shape
    t, d = 2 * t2, d2 // 2
    bm, bn = min(bm, half), min(bn, d)
    nb = d // bn
    ni = half // bm
    s_arr, s_in, s_out, s_shape = _side_specs(side, n_seq * nb * ni, lambda b, j, i: (b * nb + j) * ni + i)
    out = pl.pallas_call(
        _with_side_casts(_pos_dft_kernel, 6, len(side)),
        grid=(n_seq, nb, ni),
        in_specs=[pl.BlockSpec((bm, n), lambda b, j, i: (i, 0)),
                  pl.BlockSpec((bm, n), lambda b, j, i: (i, 0)),
                  pl.BlockSpec((None, half, bn), lambda b, j, i: (0, b, j)),
                  pl.BlockSpec((None, half, bn), lambda b, j, i: (1, b, j)),
                  pl.BlockSpec((None, half, bn), lambda b, j, i: (0, b, nb + j)),
                  pl.BlockSpec((None, half, bn), lambda b, j, i: (1, b, nb + j))] + s_in,
        out_specs=[pl.BlockSpec((None, 2, bm, bn), lambda b, j, i: (b, 0, i, j))] + s_out,
        out_shape=[jax.ShapeDtypeStruct((n_seq, 2, half, d), BF16)] + s_shape,
        compiler_params=_params("parallel", "parallel", "parallel"),
        name="pos_dft",
    )(me, mo, y2, y2, y2, y2, *s_arr)
    f = out[0].reshape(t, d)
    return (f, *out[1:]) if side else f


def _rms_prologue(z, g):
    zn = z * lax.rsqrt(jnp.mean(z * z, axis=-1, keepdims=True) + EPS)
    return (zn * g).astype(BF16)


def _q_proj_kernel(z_ref, g_ref, w_ref, cs_ref, o_ref, zn_ref):
    @pl.when(pl.program_id(1) == 0)
    def _():
        zn_ref[...] = _rms_prologue(z_ref[:, :Q_LORA], g_ref[...])

    acc = jnp.dot(zn_ref[...], w_ref[...], preferred_element_type=F32)
    cs = cs_ref[...]
    hw = QK_NOPE + 2 * QK_ROPE
    for h in range(w_ref.shape[1] // hw):
        o_ref[:, h * hw:h * hw + QK_NOPE] = acc[:, h * hw:h * hw + QK_NOPE].astype(o_ref.dtype)
        t = acc[:, h * hw + QK_NOPE:(h + 1) * hw] * cs
        r = t + pltpu.roll(t, QK_ROPE, axis=1)
        o_ref[:, h * hw + QK_NOPE:(h + 1) * hw] = r.astype(o_ref.dtype)


def q_proj(z, q_g, w_q, cs_tab, *, bm=1024, bn=2048):
    t, zw = z.shape
    n = w_q.shape[1]
    bm = min(bm, t)
    return pl.pallas_call(
        _q_proj_kernel,
        grid=(t // bm, n // bn),
        in_specs=[pl.BlockSpec((bm, zw), lambda i, j: (i, 0)),
                  pl.BlockSpec((1, Q_LORA), lambda i, j: (0, 0)),
                  pl.BlockSpec((Q_LORA, bn), lambda i, j: (0, j)),
                  pl.BlockSpec((bm, 2 * QK_ROPE), lambda i, j: (i, 0))],
        out_specs=pl.BlockSpec((bm, bn), lambda i, j: (i, j)),
        out_shape=jax.ShapeDtypeStruct((t, n), BF16),
        scratch_shapes=[pltpu.VMEM((bm, Q_LORA), BF16)],
        compiler_params=_params("parallel", "arbitrary"),
        name="q_proj",
    )(z, q_g.reshape(1, Q_LORA), w_q, cs_tab)


def _kv_proj_kernel(z_ref, g_ref, w_ref, cs_ref, o_ref, kpe_ref, zn_ref):
    @pl.when(pl.program_id(1) == 0)
    def _():
        zn_ref[...] = _rms_prologue(z_ref[:, Q_LORA:Q_LORA + KV_LORA], g_ref[...])
        t = z_ref[:, Q_LORA + KV_LORA:] * cs_ref[...]
        r = t + pltpu.roll(t, QK_ROPE, axis=1)
        lane = lax.broadcasted_iota(jnp.int32, r.shape, 1)
        kpe_ref[...] = jnp.where(lane < QK_ROPE, r, 0.0).astype(kpe_ref.dtype)

    o_ref[...] = jnp.dot(zn_ref[...], w_ref[...], preferred_element_type=F32).astype(o_ref.dtype)


def kv_proj(z, kv_g, w_kv, cs_tab, *, bm=1024, bn=2048):
    t, zw = z.shape
    n = w_kv.shape[1]
    bm = min(bm, t)
    return pl.pallas_call(
        _kv_proj_kernel,
        grid=(t // bm, n // bn),
        in_specs=[pl.BlockSpec((bm, zw), lambda i, j: (i, 0)),
                  pl.BlockSpec((1, KV_LORA), lambda i, j: (0, 0)),
                  pl.BlockSpec((KV_LORA, bn), lambda i, j: (0, j)),
                  pl.BlockSpec((bm, 2 * QK_ROPE), lambda i, j: (i, 0))],
        out_specs=[pl.BlockSpec((bm, bn), lambda i, j: (i, j)),
                   pl.BlockSpec((bm, 2 * QK_ROPE), lambda i, j: (i, 0))],
        out_shape=[jax.ShapeDtypeStruct((t, n), BF16),
                   jax.ShapeDtypeStruct((t, 2 * QK_ROPE), BF16)],
        scratch_shapes=[pltpu.VMEM((bm, KV_LORA), BF16)],
        compiler_params=_params("parallel", "arbitrary"),
        name="kv_proj",
    )(z, kv_g.reshape(1, KV_LORA), w_kv, cs_tab)


def _attn_kernel(q_ref, kc_ref, pc_ref, vc_ref, kl_ref, pl_ref, vl_ref, o_ref, *, tq, tk):
    n = q_ref.shape[0]
    n_lat = kl_ref.shape[0]
    nt = (((1,), (1,)), ((), ()))

    def key_chunks():
        yield (jnp.concatenate([kc_ref[...], pc_ref[...]], axis=1), vc_ref[...])
        for c in range(n_lat // tk):
            rows = slice(c * tk, (c + 1) * tk)
            yield (jnp.concatenate([kl_ref[rows, :], pl_ref[rows, :]], axis=1), vl_ref[rows, :])

    def lane_max(s):
        m = s[:, :LANES]
        for j in range(1, s.shape[1] // LANES):
            m = jnp.maximum(m, s[:, j * LANES:(j + 1) * LANES])
        return m

    def one_pass(q):
        ref = None
        acc = None
        top = None
        for k, v in key_chunks():
            s = lax.dot_general(q, k, nt, preferred_element_type=F32)
            if ref is None:
                ref = jnp.max(s, axis=-1, keepdims=True)
            p = jnp.exp2(s - ref).astype(BF16)
            v1 = jnp.concatenate([v, jnp.ones_like(v)], axis=1)
            part = jnp.dot(p, v1, preferred_element_type=F32)
            acc = part if acc is None else acc + part
            m = lane_max(s)
            top = m if top is None else jnp.maximum(top, m)
        return acc[:, :V_DIM] / acc[:, V_DIM:], jnp.max(top - ref)

    def online(q):
        m = jnp.full((tq, 1), NEG_INF, F32)
        l = jnp.zeros((tq, 1), F32)
        acc = jnp.zeros((tq, V_DIM), F32)
        for k, v in key_chunks():
            s = lax.dot_general(q, k, nt, preferred_element_type=F32)
            m_new = jnp.maximum(m, jnp.max(s, axis=-1, keepdims=True))
            alpha = jnp.exp2(m - m_new)
            p = jnp.exp2(s - m_new)
            l = alpha * l + jnp.sum(p, axis=-1, keepdims=True)
            acc = alpha * acc + jnp.dot(p.astype(BF16), v, preferred_element_type=F32)
            m = m_new
        return acc / l

    def q_tile(i, carry):
        r0 = pl.multiple_of(i * tq, tq)
        q = q_ref[pl.ds(r0, tq), :]
        out, excess = one_pass(q)
        o_ref[pl.ds(r0, tq), :] = out.astype(o_ref.dtype)

        @pl.when(jnp.logical_not(excess <= MAX_EXCESS))
        def _():
            o_ref[pl.ds(r0, tq), :] = online(q).astype(o_ref.dtype)

        return carry

    lax.fori_loop(0, n // tq, q_tile, 0)


def mla_attention(q, kv_c, kpe_c, kv_l, kpe_l, *, batch, heads, tq=1024, tk=512):
    n = q.shape[0] // batch
    n_ctx = kv_c.shape[0] // batch
    tq, tk = min(tq, n), min(tk, n)
    hq = QK_NOPE + 2 * QK_ROPE
    return pl.pallas_call(
        functools.partial(_attn_kernel, tq=tq, tk=tk),
        grid=(batch, heads),
        in_specs=[pl.BlockSpec((n, hq), lambda b, h: (b, h)),
                  pl.BlockSpec((n_ctx, QK_NOPE), lambda b, h: (b, 2 * h)),
                  pl.BlockSpec((n_ctx, 2 * QK_ROPE), lambda b, h: (b, 0)),
                  pl.BlockSpec((n_ctx, V_DIM), lambda b, h: (b, 2 * h + 1)),
                  pl.BlockSpec((n, QK_NOPE), lambda b, h: (b, 2 * h)),
                  pl.BlockSpec((n, 2 * QK_ROPE), lambda b, h: (b, 0)),
                  pl.BlockSpec((n, V_DIM), lambda b, h: (b, 2 * h + 1))],
        out_specs=pl.BlockSpec((n, V_DIM), lambda b, h: (b, h)),
        out_shape=jax.ShapeDtypeStruct((batch * n, heads * V_DIM), BF16),
        compiler_params=_params("parallel", "parallel"),
        name="mla_attention",
    )(q, kv_c, kpe_c, kv_c, kv_l, kpe_l, kv_l)


def _peer_scores_kernel(h_ref, wq_ref, kk_ref, o_ref):
    q = jnp.dot(h_ref[...], wq_ref[...], preferred_element_type=F32)
    nt = (((1,), (1,)), ((), ()))
    for c in range(wq_ref.shape[1] // PEER_HALF):
        qc = q[:, c * PEER_HALF:(c + 1) * PEER_HALF]
        o_ref[c] = lax.dot_general(kk_ref[c], qc, nt, precision=lax.Precision.HIGHEST,
                                   preferred_element_type=F32)


def peer_scores(h, wq, kk, *, bm=512, bn=1024):
    t, d = h.shape
    n = wq.shape[1]
    bm, bn = min(bm, t), min(bn, n)
    cb = bn // PEER_HALF
    return pl.pallas_call(
        _peer_scores_kernel,
        grid=(n // bn, t // bm),
        in_specs=[pl.BlockSpec((bm, d), lambda j, i: (i, 0)),
                  pl.BlockSpec((d, bn), lambda j, i: (0, j)),
                  pl.BlockSpec((cb, N_KEYS, PEER_HALF), lambda j, i: (j, 0, 0))],
        out_specs=pl.BlockSpec((cb, N_KEYS, bm), lambda j, i: (j, 0, i)),
        out_shape=jax.ShapeDtypeStruct((n // PEER_HALF, N_KEYS, t), F32),
        compiler_params=_params("parallel", "parallel"),
        name="peer_scores",
    )(h, wq, kk)


_HALF_K = PEER_TOPK // 2


def _top16(s, iota):
    vals = []
    rank = jnp.full(s.shape, int(NO_RANK), jnp.int32)
    for r in range(PEER_TOPK):
        m = jnp.max(s, axis=0, keepdims=True)
        idx = jnp.min(jnp.where(s == m, iota, N_KEYS), axis=0, keepdims=True)
        hit = iota == idx
        rank = jnp.where(hit, r, rank)
        s = jnp.where(hit, NEG_INF, s)
        vals.append(m)
    return vals, rank


def _extract16(arrays, with_rank):
    arrays = list(arrays)
    vals = [[] for _ in arrays]
    ranks = [jnp.full(s.shape, NO_RANK, F32) if w else None for s, w in zip(arrays, with_rank)]
    for r in range(PEER_TOPK):
        for i, s in enumerate(arrays):
            m = jnp.max(s, axis=0, keepdims=True)
            vals[i].append(m)
            hit = s == m
            if with_rank[i]:
                ranks[i] = jnp.where(hit, float(r), ranks[i])
            arrays[i] = jnp.where(hit, NEG_INF, s)
    bad = [jnp.sum(jnp.where(s == NEG_INF, 1.0, 0.0), axis=0, keepdims=True) != float(PEER_TOPK)
           for s in arrays]
    return vals, bad, ranks


def _candidates(a, b):
    a_hi = jnp.concatenate(a[_HALF_K:], axis=0)
    b_all = jnp.concatenate(b, axis=0)
    b_lo = b_all[:_HALF_K]
    return jnp.concatenate([a[0] + b_all] + [a[r] + b_lo for r in range(1, _HALF_K)]
                           + [a_hi + b[0]], axis=0)


def _peer_topk_fast(s_ref, k1_ref, k2_ref, heads):
    tb = s_ref.shape[2]
    s = [s_ref[2 * h + i] for h in heads for i in range(2)]
    tops, bad_s, ranks = _extract16(s, [False, True] * len(heads))
    cands = [_candidates(tops[2 * n], tops[2 * n + 1]) for n in range(len(heads))]
    sums, bad_c, _ = _extract16(cands, [False] * len(heads))
    for n, h in enumerate(heads):
        s1, s2, a, b, cand = s[2 * n], s[2 * n + 1], tops[2 * n], tops[2 * n + 1], cands[n]
        keep = jnp.where(cand >= sums[n][-1], 1.0, 0.0)
        z = jnp.sum(keep * jnp.exp(cand - sums[n][0]), axis=0, keepdims=True)
        lim = jnp.zeros((N_KEYS, tb), F32)
        for r, count in enumerate(_rank1_counts(keep)):
            lim = jnp.where(s1 == a[r], count, lim)
        _emit(k1_ref, k2_ref, h, lim, ranks[2 * n + 1], jnp.exp(s1 - a[0]) * (1.0 / z), jnp.exp(s2 - b[0]))
    return [bad_s[2 * n] | bad_s[2 * n + 1] | bad_c[n] for n in range(len(heads))]


def _rank1_counts(sel):
    counts = [jnp.sum(sel[:PEER_TOPK], axis=0, keepdims=True)]
    for r in range(1, _HALF_K):
        lo = PEER_TOPK + (r - 1) * _HALF_K
        counts.append(jnp.sum(sel[lo:lo + _HALF_K], axis=0, keepdims=True))
    tail = sel[PEER_TOPK + (_HALF_K - 1) * _HALF_K:]
    return counts + [tail[r:r + 1] for r in range(_HALF_K)]


def _emit(k1_ref, k2_ref, h, count, rank2, p1, p2):
    k1_ref[h, 0] = 1.0 - count
    k1_ref[h, 1] = p1
    k2_ref[h, 0] = (-rank2).astype(k2_ref.dtype)
    k2_ref[h, 1] = p2.astype(k2_ref.dtype)


def _peer_topk_kernel(s_ref, k1_ref, k2_ref):
    n_heads = s_ref.shape[0] // 2
    group = min(TOPK_GROUP, n_heads)
    for h0 in range(0, n_heads, group):
        heads = list(range(h0, h0 + group))
        for h, bad in zip(heads, _peer_topk_fast(s_ref, k1_ref, k2_ref, heads)):
            @pl.when(jnp.max(jnp.where(bad, 1.0, 0.0)) > 0.0)
            def _(h=h):
                _peer_topk_exact(s_ref, k1_ref, k2_ref, h)


def _peer_topk_exact(s_ref, k1_ref, k2_ref, h):
    tb = s_ref.shape[2]
    iota = lax.broadcasted_iota(jnp.int32, (N_KEYS, tb), 0)
    i16 = lax.broadcasted_iota(jnp.int32, (PEER_TOPK, tb), 0)
    i8 = lax.broadcasted_iota(jnp.int32, (_HALF_K, tb), 0)
    flat = jnp.concatenate([i16] + [i8 + PEER_TOPK * r for r in range(1, _HALF_K)]
                           + [(i8 + _HALF_K) * PEER_TOPK], axis=0)
    s1 = s_ref[2 * h]
    s2 = s_ref[2 * h + 1]
    a, rank1 = _top16(s1, iota)
    b, rank2 = _top16(s2, iota)
    cand = _candidates(a, b)
    cmax = a[0] + b[0]
    sel = jnp.zeros(flat.shape, jnp.bool_)
    c = cand
    for _ in range(PEER_TOPK):
        m = jnp.max(c, axis=0, keepdims=True)
        idx = jnp.min(jnp.where(c == m, flat, PEER_TOPK * PEER_TOPK), axis=0, keepdims=True)
        hit = flat == idx
        sel = jnp.logical_or(sel, hit)
        c = jnp.where(hit, NEG_INF, c)
    self = sel.astype(F32)
    z = jnp.sum(self * jnp.exp(cand - cmax), axis=0, keepdims=True)
    lim = jnp.zeros((N_KEYS, tb), F32)
    for r, count in enumerate(_rank1_counts(self)):
        lim = jnp.where(rank1 == r, count, lim)
    _emit(k1_ref, k2_ref, h, lim, rank2.astype(F32), jnp.exp(s1 - a[0]) * (1.0 / z), jnp.exp(s2 - b[0]))


def peer_topk(scores, *, tb=256):
    c2, nk, t = scores.shape
    tb = min(tb, t)
    hh = c2 // 2
    return pl.pallas_call(
        _peer_topk_kernel,
        grid=(t // tb,),
        in_specs=[pl.BlockSpec((c2, nk, tb), lambda i: (0, 0, i))],
        out_specs=[pl.BlockSpec((hh, 2, nk, tb), lambda i: (0, 0, 0, i)),
                   pl.BlockSpec((hh, 2, nk, tb), lambda i: (0, 0, 0, i))],
        out_shape=[jax.ShapeDtypeStruct((hh, 2, nk, t), F32),
                   jax.ShapeDtypeStruct((hh, 2, nk, t), BF16)],
        compiler_params=_params("parallel"),
        name="peer_topk",
    )(scores)


def _gelu(x):
    return 0.5 * x * (1.0 + lax.erf(x * (1.0 / math.sqrt(2.0))))


def _peer_dense_kernel(h_ref, u_ref, v_ref, k1_ref, k2_ref, o_ref):
    j = pl.program_id(1)
    n1 = u_ref.shape[0] // N_KEYS
    tb = h_ref.shape[0]
    nt = (((1,), (1,)), ((), ()))
    tn = (((0,), (0,)), ((), ()))
    odd = (j % 2) == 1

    @pl.when(j == 0)
    def _():
        o_ref[...] = jnp.zeros_like(o_ref)

    a_t = lax.dot_general(u_ref[...], h_ref[...], nt, preferred_element_type=F32)
    parts = []
    for c in range(n1):
        g = jnp.zeros((N_KEYS, tb), BF16)
        for h in range(k1_ref.shape[0]):
            lo, hi = k1_ref[h, :, c:c + 1, :], k1_ref[h, :, n1 + c:n1 + c + 1, :]
            thr = jnp.broadcast_to(jnp.where(odd, hi[0], lo[0]), (N_KEYS, tb)).astype(BF16)
            p1 = jnp.broadcast_to(jnp.where(odd, hi[1], lo[1]), (N_KEYS, tb)).astype(BF16)
            g = g + jnp.where(k2_ref[h, 0] >= thr, k2_ref[h, 1] * p1, jnp.zeros((), BF16))
        parts.append(g)
    gate = jnp.concatenate(parts, axis=0)
    w_t = gate * _gelu(a_t).astype(BF16)
    o_ref[...] += lax.dot_general(w_t, v_ref[...], tn, preferred_element_type=F32)


def peer_dense(h, u, v, key1, key2, *, tb=512, ec=512, side=()):
    t, d = h.shape
    e = u.shape[0]
    tb = min(tb, t)
    hh, _, nk, _ = key1.shape
    assert 2 * (ec // nk) == SUBLANES
    nj = e // ec
    s_arr, s_in, s_out, s_shape = _side_specs(side, (t // tb) * nj, lambda i, j: i * nj + j)
    out = pl.pallas_call(
        _with_side_casts(_peer_dense_kernel, 5, len(side)),
        grid=(t // tb, nj),
        in_specs=[pl.BlockSpec((tb, d), lambda i, j: (i, 0)),
                  pl.BlockSpec((ec, d), lambda i, j: (j, 0)),
                  pl.BlockSpec((ec, d), lambda i, j: (j, 0)),
                  pl.BlockSpec((hh, 2, SUBLANES, tb), lambda i, j: (0, 0, j // 2, i)),
                  pl.BlockSpec((hh, 2, nk, tb), lambda i, j: (0, 0, 0, i))] + s_in,
        out_specs=[pl.BlockSpec((tb, d), lambda i, j: (i, 0))] + s_out,
        out_shape=[jax.ShapeDtypeStruct((t, d), F32)] + s_shape,
        compiler_params=_params("parallel", "arbitrary"),
        name="peer_dense",
    )(h, u, v, key1, key2, *s_arr)
    return tuple(out) if side else out[0]


def peer(h, wq, kk, u, v, side=()):
    scores = peer_scores(h, wq, kk)
    key1, key2 = peer_topk(scores)
    return peer_dense(h, u, v, key1, key2, side=side)


def _rope_tables(n):
    rows = n // GRID_W
    row_id = np.repeat(np.arange(rows), GRID_W).astype(np.float64)
    col_id = np.tile(np.arange(GRID_W), rows).astype(np.float64)
    axis_dim = QK_ROPE // 2
    inv_freq = ROPE_THETA ** (-np.arange(0, axis_dim, 2, dtype=np.float64) / axis_dim)
    ang = np.stack([row_id[:, None] * inv_freq, col_id[:, None] * inv_freq], axis=1)
    cos = np.cos(ang)
    sin = np.sin(ang)
    cos_full = np.concatenate([cos, cos], axis=-1).reshape(n, QK_ROPE)
    sin_full = np.concatenate([sin, sin], axis=-1).reshape(n, QK_ROPE)
    return np.concatenate([cos_full, sin_full], axis=-1).astype(np.float32)


def _rot_partner(w):
    ws = w.reshape(w.shape[:-1] + (2, 2, QK_ROPE // 4))
    return jnp.stack([-ws[..., 1, :], ws[..., 0, :]], axis=-2).reshape(w.shape)


def kernel(x, c, ctx, c_ctx, ada_w, ada_b, norm1_g, norm2_g, final_g, fnet_wo,
           mla_w_in, mla_q_g, mla_w_uq, mla_kv_g, mla_w_ukv, mla_wo,
           peer_wq, peer_k1, peer_k2, peer_u, peer_v):
    b, n, d = x.shape
    n_ctx = ctx.shape[1]
    depth = ada_w.shape[0]
    gd = d // F_GROUPS

    xl = x.reshape(b * n, d)
    xc = ctx.reshape(b * n_ctx, d)

    ct = jnp.zeros((d, SUBLANES), F32).at[:, :b].set(c.T).at[:, b].set(c_ctx)

    def sel_lat(i, bm):
        return (i * bm) // n

    def sel_ctx(i, bm):
        return b

    cc, sc = _dft_mats(gd)
    wcs = jnp.asarray(np.concatenate([cc, sc], axis=1)).astype(BF16)
    me_l, mo_l = (jnp.asarray(m).astype(BF16) for m in _pos_dft_mats(n))
    me_c, mo_c = (jnp.asarray(m).astype(BF16) for m in _pos_dft_mats(n_ctx))
    cs_lat = jnp.asarray(np.tile(_rope_tables(n), (b, 1)))
    cs_ctx = jnp.asarray(np.tile(np.concatenate([np.ones((1, QK_ROPE), np.float32),
                                                 np.zeros((1, QK_ROPE), np.float32)], axis=1),
                                 (b * n_ctx, 1)))
    q_scale = float((QK_NOPE + QK_ROPE) ** -0.5 * math.log2(math.e))

    pend_l = None
    pend_c = None
    uv = None
    for i in range(depth):
        last = i == depth - 1
        mod = ada_linear(ct, ada_w, ada_b, i, rows=b + 1)
        mod = mod.reshape(SUBLANES, ADA_CHUNKS, 1, d)

        fourier = i % 2 == 0
        if pend_l is None:
            hl = norm_block(xl, norm1_g[i], sel_lat, mod=(mod, (0, 1)), pair_rows=fourier)
            hc = norm_block(xc, norm1_g[i], sel_ctx, mod=(mod, (0, 1)), pair_rows=fourier)
        else:
            xl, hl = norm_block(xl, norm1_g[i], sel_lat, res=pend_l, mod=(mod, (0, 1)), write_x=True)
            xc, hc = norm_block(xc, norm1_g[i], sel_ctx, res=pend_c, mod=(mod, (0, 1)), write_x=True)
            if fourier:
                hl, hc = (h.reshape(h.shape[0] // 2, 2 * d) for h in (hl, hc))
        j = i // 2
        if fourier:
            wo = cast_layer(fnet_wo, j)
            if uv is None:
                fl, u_bf = pos_dft(me_l, mo_l, chan_dft(hl, wcs), b, side=[(peer_u, i)])
                yl, v_bf = matmul(fl, wo, side=[(peer_v, i)])
                uv = (u_bf, v_bf)
            else:
                fl = pos_dft(me_l, mo_l, chan_dft(hl, wcs), b)
                yl = matmul(fl, wo)
            if not last:
                fc = pos_dft(me_c, mo_c, chan_dft(hc, wcs), b)
                yc = matmul(fc, wo)
        else:
            w_in = mla_w_in[j]
            w_in = jnp.concatenate([w_in, _rot_partner(w_in[:, Q_LORA + KV_LORA:])], axis=1).astype(BF16)
            wq3 = mla_w_uq[j].reshape(Q_LORA, MLA_HEADS, QK_NOPE + QK_ROPE)
            wq_ext = jnp.concatenate([wq3, _rot_partner(wq3[..., QK_NOPE:])], axis=-1)
            wq_ext = wq_ext.reshape(Q_LORA, MLA_HEADS * (QK_NOPE + 2 * QK_ROPE)).astype(BF16)
            w_kv = cast_layer(mla_w_ukv, j)
            wo = cast_layer(mla_wo, j)
            zl = matmul(hl, w_in, out_dtype=F32, bn=w_in.shape[1])
            zc = matmul(hc, w_in, out_dtype=F32, bn=w_in.shape[1])
            q = q_proj(zl, mla_q_g[j] * q_scale, wq_ext, cs_lat)
            kv_l, kpe_l = kv_proj(zl, mla_kv_g[j], w_kv, cs_lat)
            kv_c, kpe_c = kv_proj(zc, mla_kv_g[j], w_kv, cs_ctx)
            o = mla_attention(q, kv_c, kpe_c, kv_l, kpe_l, batch=b, heads=MLA_HEADS)
            yl = matmul(o, wo)
            if not last:
                raise NotImplementedError("context output of an MLA layer that is not last")
        wq = cast_layer(peer_wq, i)
        kk = jnp.stack([peer_k1[i], peer_k2[i]], axis=1).reshape(2 * PEER_HEADS, N_KEYS, PEER_HALF)
        u, v = uv if uv is not None else (cast_layer(peer_u, i), cast_layer(peer_v, i))
        xl, h2l = norm_block(xl, norm2_g[i], sel_lat, res=(yl, mod, 2), mod=(mod, (3, 4)), write_x=True)
        if last:
            fl = peer(h2l, wq, kk, u, v)
        else:
            fl, *uv = peer(h2l, wq, kk, u, v, side=[(peer_u, i + 1), (peer_v, i + 1)])
        pend_l = (fl, mod, 5)
        if not last:
            xc, h2c = norm_block(xc, norm2_g[i], sel_ctx, res=(yc, mod, 2), mod=(mod, (3, 4)), write_x=True)
            pend_c = (peer(h2c, wq, kk, u, v), mod, 5)

    out = norm_block(xl, final_g, sel_lat, res=pend_l, out_dtype=F32)
    return out.reshape(b, n, d)
```

```python
import functools
import math

import numpy as np
import jax
import jax.numpy as jnp
from jax import lax
from jax.experimental import pallas as pl
from jax.experimental.pallas import tpu as pltpu

F32 = jnp.float32
BF16 = jnp.bfloat16

EPS = 1e-6
ADA_CHUNKS = 6
F_GROUPS = 8
GRID_W = 64
MLA_HEADS = 32
QK_NOPE = 128
QK_ROPE = 64
V_DIM = 128
Q_LORA = 1024
KV_LORA = 512
ROPE_THETA = 10000.0
PEER_HEADS = 8
PEER_HALF = 128
N_KEYS = 128
PEER_TOPK = 16

LANES = 128
SUBLANES = 8
VMEM_LIMIT = 56 * 1024 * 1024
NEG_INF = float("-inf")
TOPK_GROUP = 4
NO_RANK = 255.0
MAX_EXCESS = 64.0


def _params(*sem):
    return pltpu.CompilerParams(dimension_semantics=sem, vmem_limit_bytes=VMEM_LIMIT)


def _ada_kernel(ct_ref, w_ref, b_ref, o_ref, sb_ref, *, kc, rows):
    d, bn = w_ref.shape
    nt = bn // LANES

    @pl.when(pl.program_id(0) == 0)
    def _():
        c = ct_ref[...]
        s = c * jax.nn.sigmoid(c)
        for r in range(rows):
            sb_ref[r] = jnp.broadcast_to(s[:, r:r + 1], (d, LANES))

    def body(i, accs):
        k0 = pl.multiple_of(i * kc, kc)
        out = list(accs)
        for t in range(nt):
            wv = w_ref[pl.ds(k0, kc), t * LANES:(t + 1) * LANES]
            for r in range(rows):
                p = wv * sb_ref[r, pl.ds(k0, kc), :]
                out[r * nt + t] = out[r * nt + t] + p.reshape(kc // SUBLANES, SUBLANES, LANES).sum(axis=0)
        return tuple(out)

    init = tuple(jnp.zeros((SUBLANES, LANES), F32) for _ in range(rows * nt))
    accs = lax.fori_loop(0, d // kc, body, init)
    o_ref[...] = jnp.zeros_like(o_ref)
    for r in range(rows):
        for t in range(nt):
            row = accs[r * nt + t].sum(axis=0, keepdims=True) + b_ref[:, t * LANES:(t + 1) * LANES]
            o_ref[r:r + 1, t * LANES:(t + 1) * LANES] = row


def ada_linear(ct, w, b, layer, *, rows, bn=1024, kc=256):
    nl, d, n = w.shape
    bn = min(bn, n)
    kc = min(kc, d)
    return pl.pallas_call(
        functools.partial(_ada_kernel, kc=kc, rows=rows),
        grid=(n // bn,),
        in_specs=[
            pl.BlockSpec((d, SUBLANES), lambda j: (0, 0)),
            pl.BlockSpec((None, d, bn), lambda j: (layer, 0, j)),
            pl.BlockSpec((None, 1, bn), lambda j: (layer, 0, j)),
        ],
        out_specs=pl.BlockSpec((SUBLANES, bn), lambda j: (0, j)),
        out_shape=jax.ShapeDtypeStruct((SUBLANES, n), F32),
        scratch_shapes=[pltpu.VMEM((rows, d, LANES), F32)],
        compiler_params=_params("arbitrary"),
        name="ada_linear",
    )(ct, w, b.reshape(nl, 1, n))


def _cast_kernel(w_ref, o_ref):
    o_ref[...] = w_ref[...].astype(o_ref.dtype)


def cast_layer(w, layer, *, bm=512):
    _, r, c = w.shape
    bm = min(bm, r)
    return pl.pallas_call(
        _cast_kernel,
        grid=(r // bm,),
        in_specs=[pl.BlockSpec((None, bm, c), lambda i: (layer, i, 0))],
        out_specs=pl.BlockSpec((bm, c), lambda i: (i, 0)),
        out_shape=jax.ShapeDtypeStruct((r, c), BF16),
        compiler_params=_params("parallel"),
        name="cast_layer",
    )(w)


def _norm_kernel(*refs, has_res, has_mod, write_x, pair_rows):
    it = iter(refs)
    x_ref = next(it)
    if has_res:
        y_ref, gate_ref = next(it), next(it)
    g_ref = next(it)
    if has_mod:
        shift_ref, scale_ref = next(it), next(it)
    if write_x:
        xo_ref = next(it)
    h_ref = next(it)

    def normed(x):
        y = x * lax.rsqrt(jnp.mean(x * x, axis=-1, keepdims=True) + EPS)
        y = y * g_ref[...]
        if has_mod:
            y = y * (1.0 + scale_ref[...]) + shift_ref[...]
        return y.astype(h_ref.dtype)

    if pair_rows:
        bm, d = x_ref.shape
        h = normed(x_ref[...])
        r = lax.broadcasted_iota(jnp.int32, (bm // 2, bm), 0)
        c = lax.broadcasted_iota(jnp.int32, (bm // 2, bm), 1)
        for p in range(2):
            pick = jnp.where(c == 2 * r + p, 1.0, 0.0).astype(h.dtype)
            h_ref[:, p * d:(p + 1) * d] = jnp.dot(pick, h, preferred_element_type=F32).astype(h_ref.dtype)
        return

    x = x_ref[...]
    if has_res:
        x = x + gate_ref[...] * y_ref[...].astype(F32)
    if write_x:
        xo_ref[...] = x
    h_ref[...] = normed(x)


def norm_block(x, g, sel, *, res=None, mod=None, write_x=False, pair_rows=False, out_dtype=BF16, bm=256):
    t, d = x.shape
    bm = min(bm, t)
    assert not (pair_rows and (res is not None or write_x))
    row = pl.BlockSpec((bm, d), lambda i: (i, 0))
    h_shape, h_block = ((t // 2, 2 * d), (bm // 2, 2 * d)) if pair_rows else ((t, d), (bm, d))

    def mod_spec(a):
        return pl.BlockSpec((None, None, 1, d), lambda i, a=a: (sel(i, bm), a, 0, 0))

    args, specs = [x], [row]
    if res is not None:
        args += [res[0], res[1]]
        specs += [row, mod_spec(res[2])]
    args.append(g.reshape(1, d))
    specs.append(pl.BlockSpec((1, d), lambda i: (0, 0)))
    if mod is not None:
        args += [mod[0], mod[0]]
        specs += [mod_spec(mod[1][0]), mod_spec(mod[1][1])]
    out_shape, out_specs = [], []
    if write_x:
        out_shape.append(jax.ShapeDtypeStruct((t, d), F32))
        out_specs.append(row)
    out_shape.append(jax.ShapeDtypeStruct(h_shape, out_dtype))
    out_specs.append(pl.BlockSpec(h_block, lambda i: (i, 0)))
    out = pl.pallas_call(
        functools.partial(_norm_kernel, has_res=res is not None, has_mod=mod is not None, write_x=write_x,
                          pair_rows=pair_rows),
        grid=(t // bm,),
        in_specs=specs,
        out_specs=out_specs,
        out_shape=out_shape,
        compiler_params=_params("parallel"),
        name="norm_block",
    )(*args)
    return out if write_x else out[0]


def _side_specs(side, n_steps, step_of):
    arrays, ins, outs, shapes = [], [], [], []
    for w, layer in side:
        _, r, c = w.shape
        rows = r // n_steps
        assert rows * n_steps == r and rows % (2 * SUBLANES) == 0
        arrays.append(w)
        ins.append(pl.BlockSpec((None, rows, c), lambda *g, layer=layer: (layer, step_of(*g), 0)))
        outs.append(pl.BlockSpec((rows, c), lambda *g: (step_of(*g), 0)))
        shapes.append(jax.ShapeDtypeStruct((r, c), BF16))
    return arrays, ins, outs, shapes


def _with_side_casts(body, n_in, n_side):
    def kernel(*refs):
        main_in, side_in = refs[:n_in], refs[n_in:n_in + n_side]
        rest = refs[n_in + n_side:]
        main_out, side_out = rest[:len(rest) - n_side], rest[len(rest) - n_side:]
        body(*main_in, *main_out)
        for src, dst in zip(side_in, side_out):
            dst[...] = src[...].astype(dst.dtype)
    return kernel


def _mm_kernel(a_ref, b_ref, o_ref):
    o_ref[...] = jnp.dot(a_ref[...], b_ref[...], preferred_element_type=F32).astype(o_ref.dtype)


def matmul(a, b, *, out_dtype=BF16, bm=512, bn=1024, side=()):
    m, k = a.shape
    _, n = b.shape
    bm, bn = min(bm, m), min(bn, n)
    ni = m // bm
    s_arr, s_in, s_out, s_shape = _side_specs(side, (n // bn) * ni, lambda j, i: j * ni + i)
    out = pl.pallas_call(
        _with_side_casts(_mm_kernel, 2, len(side)),
        grid=(n // bn, ni),
        in_specs=[pl.BlockSpec((bm, k), lambda j, i: (i, 0)),
                  pl.BlockSpec((k, bn), lambda j, i: (0, j))] + s_in,
        out_specs=[pl.BlockSpec((bm, bn), lambda j, i: (i, j))] + s_out,
        out_shape=[jax.ShapeDtypeStruct((m, n), out_dtype)] + s_shape,
        compiler_params=_params("parallel", "parallel"),
        name="matmul",
    )(a, b, *s_arr)
    return tuple(out) if side else out[0]


def _dft_mats(n):
    k = np.arange(n, dtype=np.int64)
    ang = 2.0 * np.pi * ((k[:, None] * k[None, :]) % n).astype(np.float64) / n
    s = 1.0 / math.sqrt(n)
    return (np.cos(ang) * s).astype(np.float32), (np.sin(ang) * s).astype(np.float32)


def _chan_dft_kernel(h_ref, w_ref, o_ref):
    gd = h_ref.shape[1]
    y = jnp.dot(h_ref[...], w_ref[...], preferred_element_type=F32)
    o_ref[0] = y[:, :gd].astype(o_ref.dtype)
    o_ref[1] = y[:, gd:].astype(o_ref.dtype)


def chan_dft(h, wcs, *, bm=1024):
    t, d = h.shape
    gd = wcs.shape[0]
    bm = min(bm, t)
    return pl.pallas_call(
        _chan_dft_kernel,
        grid=(t // bm, d // gd),
        in_specs=[pl.BlockSpec((bm, gd), lambda i, g: (i, g)),
                  pl.BlockSpec((gd, 2 * gd), lambda i, g: (0, 0))],
        out_specs=pl.BlockSpec((2, bm, gd), lambda i, g: (0, i, g)),
        out_shape=jax.ShapeDtypeStruct((2, t, d), BF16),
        compiler_params=_params("parallel", "parallel"),
        name="chan_dft",
    )(h, wcs)


def _pos_dft_mats(n):
    cn, sn = _dft_mats(n)
    half = n // 2
    me = np.concatenate([cn[:half, 0::2], -sn[:half, 0::2]], axis=1)
    mo = np.concatenate([cn[:half, 1::2], -sn[:half, 1::2]], axis=1)
    return me, mo


def _pos_dft_kernel(me_ref, mo_ref, yce_ref, yse_ref, yco_ref, yso_ref, o_ref):
    ye = jnp.concatenate([yce_ref[...], yse_ref[...]], axis=0)
    yo = jnp.concatenate([yco_ref[...], yso_ref[...]], axis=0)
    e = jnp.dot(me_ref[...], ye, preferred_element_type=F32)
    o = jnp.dot(mo_ref[...], yo, preferred_element_type=F32)
    o_ref[0] = (e + o).astype(o_ref.dtype)
    o_ref[1] = (e - o).astype(o_ref.dtype)


def pos_dft(me, mo, y2, n_seq, *, bm=512, bn=512, side=()):
    half, n = me.shape
    _, t2, d2 = y2.shape
    t, d = 2 * t2, d2 // 2
    bm, bn = min(bm, half), min(bn, d)
    nb = d // bn
    ni = half // bm
    s_arr, s_in, s_out, s_shape = _side_specs(side, n_seq * nb * ni, lambda b, j, i: (b * nb + j) * ni + i)
    out = pl.pallas_call(
        _with_side_casts(_pos_dft_kernel, 6, len(side)),
        grid=(n_seq, nb, ni),
        in_specs=[pl.BlockSpec((bm, n), lambda b, j, i: (i, 0)),
                  pl.BlockSpec((bm, n), lambda b, j, i: (i, 0)),
                  pl.BlockSpec((None, half, bn), lambda b, j, i: (0, b, j)),
                  pl.BlockSpec((None, half, bn), lambda b, j, i: (1, b, j)),
                  pl.BlockSpec((None, half, bn), lambda b, j, i: (0, b, nb + j)),
                  pl.BlockSpec((None, half, bn), lambda b, j, i: (1, b, nb + j))] + s_in,
        out_specs=[pl.BlockSpec((None, 2, bm, bn), lambda b, j, i: (b, 0, i, j))] + s_out,
        out_shape=[jax.ShapeDtypeStruct((n_seq, 2, half, d), BF16)] + s_shape,
        compiler_params=_params("parallel", "parallel", "parallel"),
        name="pos_dft",
    )(me, mo, y2, y2, y2, y2, *s_arr)
    f = out[0].reshape(t, d)
    return (f, *out[1:]) if side else f


def _rms_prologue(z, g):
    zn = z * lax.rsqrt(jnp.mean(z * z, axis=-1, keepdims=True) + EPS)
    return (zn * g).astype(BF16)


def _q_proj_kernel(z_ref, g_ref, w_ref, cs_ref, o_ref, zn_ref):
    @pl.when(pl.program_id(1) == 0)
    def _():
        zn_ref[...] = _rms_prologue(z_ref[:, :Q_LORA], g_ref[...])

    acc = jnp.dot(zn_ref[...], w_ref[...], preferred_element_type=F32)
    cs = cs_ref[...]
    hw = QK_NOPE + 2 * QK_ROPE
    for h in range(w_ref.shape[1] // hw):
        o_ref[:, h * hw:h * hw + QK_NOPE] = acc[:, h * hw:h * hw + QK_NOPE].astype(o_ref.dtype)
        t = acc[:, h * hw + QK_NOPE:(h + 1) * hw] * cs
        r = t + pltpu.roll(t, QK_ROPE, axis=1)
        o_ref[:, h * hw + QK_NOPE:(h + 1) * hw] = r.astype(o_ref.dtype)


def q_proj(z, q_g, w_q, cs_tab, *, bm=1024, bn=2048):
    t, zw = z.shape
    n = w_q.shape[1]
    bm = min(bm, t)
    return pl.pallas_call(
        _q_proj_kernel,
        grid=(t // bm, n // bn),
        in_specs=[pl.BlockSpec((bm, zw), lambda i, j: (i, 0)),
                  pl.BlockSpec((1, Q_LORA), lambda i, j: (0, 0)),
                  pl.BlockSpec((Q_LORA, bn), lambda i, j: (0, j)),
                  pl.BlockSpec((bm, 2 * QK_ROPE), lambda i, j: (i, 0))],
        out_specs=pl.BlockSpec((bm, bn), lambda i, j: (i, j)),
        out_shape=jax.ShapeDtypeStruct((t, n), BF16),
        scratch_shapes=[pltpu.VMEM((bm, Q_LORA), BF16)],
        compiler_params=_params("parallel", "arbitrary"),
        name="q_proj",
    )(z, q_g.reshape(1, Q_LORA), w_q, cs_tab)


def _kv_proj_kernel(z_ref, g_ref, w_ref, cs_ref, o_ref, kpe_ref, zn_ref):
    @pl.when(pl.program_id(1) == 0)
    def _():
        zn_ref[...] = _rms_prologue(z_ref[:, Q_LORA:Q_LORA + KV_LORA], g_ref[...])
        t = z_ref[:, Q_LORA + KV_LORA:] * cs_ref[...]
        r = t + pltpu.roll(t, QK_ROPE, axis=1)
        lane = lax.broadcasted_iota(jnp.int32, r.shape, 1)
        kpe_ref[...] = jnp.where(lane < QK_ROPE, r, 0.0).astype(kpe_ref.dtype)

    o_ref[...] = jnp.dot(zn_ref[...], w_ref[...], preferred_element_type=F32).astype(o_ref.dtype)


def kv_proj(z, kv_g, w_kv, cs_tab, *, bm=1024, bn=2048):
    t, zw = z.shape
    n = w_kv.shape[1]
    bm = min(bm, t)
    return pl.pallas_call(
        _kv_proj_kernel,
        grid=(t // bm, n // bn),
        in_specs=[pl.BlockSpec((bm, zw), lambda i, j: (i, 0)),
                  pl.BlockSpec((1, KV_LORA), lambda i, j: (0, 0)),
                  pl.BlockSpec((KV_LORA, bn), lambda i, j: (0, j)),
                  pl.BlockSpec((bm, 2 * QK_ROPE), lambda i, j: (i, 0))],
        out_specs=[pl.BlockSpec((bm, bn), lambda i, j: (i, j)),
                   pl.BlockSpec((bm, 2 * QK_ROPE), lambda i, j: (i, 0))],
        out_shape=[jax.ShapeDtypeStruct((t, n), BF16),
                   jax.ShapeDtypeStruct((t, 2 * QK_ROPE), BF16)],
        scratch_shapes=[pltpu.VMEM((bm, KV_LORA), BF16)],
        compiler_params=_params("parallel", "arbitrary"),
        name="kv_proj",
    )(z, kv_g.reshape(1, KV_LORA), w_kv, cs_tab)


def _attn_kernel(q_ref, kc_ref, pc_ref, vc_ref, kl_ref, pl_ref, vl_ref, o_ref, *, tq, tk):
    n = q_ref.shape[0]
    n_lat = kl_ref.shape[0]
    nt = (((1,), (1,)), ((), ()))

    def key_chunks():
        yield (jnp.concatenate([kc_ref[...], pc_ref[...]], axis=1), vc_ref[...])
        for c in range(n_lat // tk):
            rows = slice(c * tk, (c + 1) * tk)
            yield (jnp.concatenate([kl_ref[rows, :], pl_ref[rows, :]], axis=1), vl_ref[rows, :])

    def lane_max(s):
        m = s[:, :LANES]
        for j in range(1, s.shape[1] // LANES):
            m = jnp.maximum(m, s[:, j * LANES:(j + 1) * LANES])
        return m

    def one_pass(q):
        ref = None
        acc = None
        top = None
        for k, v in key_chunks():
            s = lax.dot_general(q, k, nt, preferred_element_type=F32)
            if ref is None:
                ref = jnp.max(s, axis=-1, keepdims=True)
            p = jnp.exp2(s - ref).astype(BF16)
            v1 = jnp.concatenate([v, jnp.ones_like(v)], axis=1)
            part = jnp.dot(p, v1, preferred_element_type=F32)
            acc = part if acc is None else acc + part
            m = lane_max(s)
            top = m if top is None else jnp.maximum(top, m)
        return acc[:, :V_DIM] / acc[:, V_DIM:], jnp.max(top - ref)

    def online(q):
        m = jnp.full((tq, 1), NEG_INF, F32)
        l = jnp.zeros((tq, 1), F32)
        acc = jnp.zeros((tq, V_DIM), F32)
        for k, v in key_chunks():
            s = lax.dot_general(q, k, nt, preferred_element_type=F32)
            m_new = jnp.maximum(m, jnp.max(s, axis=-1, keepdims=True))
            alpha = jnp.exp2(m - m_new)
            p = jnp.exp2(s - m_new)
            l = alpha * l + jnp.sum(p, axis=-1, keepdims=True)
            acc = alpha * acc + jnp.dot(p.astype(BF16), v, preferred_element_type=F32)
            m = m_new
        return acc / l

    def q_tile(i, carry):
        r0 = pl.multiple_of(i * tq, tq)
        q = q_ref[pl.ds(r0, tq), :]
        out, excess = one_pass(q)
        o_ref[pl.ds(r0, tq), :] = out.astype(o_ref.dtype)

        @pl.when(jnp.logical_not(excess <= MAX_EXCESS))
        def _():
            o_ref[pl.ds(r0, tq), :] = online(q).astype(o_ref.dtype)

        return carry

    lax.fori_loop(0, n // tq, q_tile, 0)


def mla_attention(q, kv_c, kpe_c, kv_l, kpe_l, *, batch, heads, tq=1024, tk=512):
    n = q.shape[0] // batch
    n_ctx = kv_c.shape[0] // batch
    tq, tk = min(tq, n), min(tk, n)
    hq = QK_NOPE + 2 * QK_ROPE
    return pl.pallas_call(
        functools.partial(_attn_kernel, tq=tq, tk=tk),
        grid=(batch, heads),
        in_specs=[pl.BlockSpec((n, hq), lambda b, h: (b, h)),
                  pl.BlockSpec((n_ctx, QK_NOPE), lambda b, h: (b, 2 * h)),
                  pl.BlockSpec((n_ctx, 2 * QK_ROPE), lambda b, h: (b, 0)),
                  pl.BlockSpec((n_ctx, V_DIM), lambda b, h: (b, 2 * h + 1)),
                  pl.BlockSpec((n, QK_NOPE), lambda b, h: (b, 2 * h)),
                  pl.BlockSpec((n, 2 * QK_ROPE), lambda b, h: (b, 0)),
                  pl.BlockSpec((n, V_DIM), lambda b, h: (b, 2 * h + 1))],
        out_specs=pl.BlockSpec((n, V_DIM), lambda b, h: (b, h)),
        out_shape=jax.ShapeDtypeStruct((batch * n, heads * V_DIM), BF16),
        compiler_params=_params("parallel", "parallel"),
        name="mla_attention",
    )(q, kv_c, kpe_c, kv_c, kv_l, kpe_l, kv_l)


def _peer_scores_kernel(h_ref, wq_ref, kk_ref, o_ref):
    q = jnp.dot(h_ref[...], wq_ref[...], preferred_element_type=F32)
    nt = (((1,), (1,)), ((), ()))
    q_hi = q.astype(BF16)
    q_lo = (q - q_hi.astype(F32)).astype(BF16)
    for c in range(wq_ref.shape[1] // PEER_HALF):
        cols = slice(c * PEER_HALF, (c + 1) * PEER_HALF)
        k = kk_ref[c]
        k_hi = k.astype(BF16)
        k_lo = (k - k_hi.astype(F32)).astype(BF16)
        s = lax.dot_general(k_hi, q_hi[:, cols], nt, preferred_element_type=F32)
        s = s + lax.dot_general(k_lo, q_hi[:, cols], nt, preferred_element_type=F32)
        s = s + lax.dot_general(k_hi, q_lo[:, cols], nt, preferred_element_type=F32)
        o_ref[c] = s


def peer_scores(h, wq, kk, *, bm=512, bn=1024):
    t, d = h.shape
    n = wq.shape[1]
    bm, bn = min(bm, t), min(bn, n)
    cb = bn // PEER_HALF
    return pl.pallas_call(
        _peer_scores_kernel,
        grid=(n // bn, t // bm),
        in_specs=[pl.BlockSpec((bm, d), lambda j, i: (i, 0)),
                  pl.BlockSpec((d, bn), lambda j, i: (0, j)),
                  pl.BlockSpec((cb, N_KEYS, PEER_HALF), lambda j, i: (j, 0, 0))],
        out_specs=pl.BlockSpec((cb, N_KEYS, bm), lambda j, i: (j, 0, i)),
        out_shape=jax.ShapeDtypeStruct((n // PEER_HALF, N_KEYS, t), F32),
        compiler_params=_params("parallel", "parallel"),
        name="peer_scores",
    )(h, wq, kk)


_HALF_K = PEER_TOPK // 2


def _top16(s, iota):
    vals = []
    rank = jnp.full(s.shape, int(NO_RANK), jnp.int32)
    for r in range(PEER_TOPK):
        m = jnp.max(s, axis=0, keepdims=True)
        idx = jnp.min(jnp.where(s == m, iota, N_KEYS), axis=0, keepdims=True)
        hit = iota == idx
        rank = jnp.where(hit, r, rank)
        s = jnp.where(hit, NEG_INF, s)
        vals.append(m)
    return vals, rank


def _extract16(arrays, with_rank):
    arrays = list(arrays)
    vals = [[] for _ in arrays]
    ranks = [jnp.full(s.shape, NO_RANK, F32) if w else None for s, w in zip(arrays, with_rank)]
    for r in range(PEER_TOPK):
        for i, s in enumerate(arrays):
            m = jnp.max(s, axis=0, keepdims=True)
            vals[i].append(m)
            hit = s == m
            if with_rank[i]:
                ranks[i] = jnp.where(hit, float(r), ranks[i])
            arrays[i] = jnp.where(hit, NEG_INF, s)
    bad = [jnp.sum(jnp.where(s == NEG_INF, 1.0, 0.0), axis=0, keepdims=True) != float(PEER_TOPK)
           for s in arrays]
    return vals, bad, ranks


def _candidates(a, b):
    a_hi = jnp.concatenate(a[_HALF_K:], axis=0)
    b_all = jnp.concatenate(b, axis=0)
    b_lo = b_all[:_HALF_K]
    return jnp.concatenate([a[0] + b_all] + [a[r] + b_lo for r in range(1, _HALF_K)]
                           + [a_hi + b[0]], axis=0)


def _peer_topk_fast(s_ref, k1_ref, k2_ref, heads):
    tb = s_ref.shape[2]
    s = [s_ref[2 * h + i] for h in heads for i in range(2)]
    tops, bad_s, ranks = _extract16(s, [False, True] * len(heads))
    cands = [_candidates(tops[2 * n], tops[2 * n + 1]) for n in range(len(heads))]
    sums, bad_c, _ = _extract16(cands, [False] * len(heads))
    for n, h in enumerate(heads):
        s1, s2, a, b, cand = s[2 * n], s[2 * n + 1], tops[2 * n], tops[2 * n + 1], cands[n]
        keep = jnp.where(cand >= sums[n][-1], 1.0, 0.0)
        z = jnp.sum(keep * jnp.exp(cand - sums[n][0]), axis=0, keepdims=True)
        lim = jnp.zeros((N_KEYS, tb), F32)
        for r, count in enumerate(_rank1_counts(keep)):
            lim = jnp.where(s1 == a[r], count, lim)
        _emit(k1_ref, k2_ref, h, lim, ranks[2 * n + 1], jnp.exp(s1 - a[0]) * (1.0 / z), jnp.exp(s2 - b[0]))
    return [bad_s[2 * n] | bad_s[2 * n + 1] | bad_c[n] for n in range(len(heads))]


def _rank1_counts(sel):
    counts = [jnp.sum(sel[:PEER_TOPK], axis=0, keepdims=True)]
    for r in range(1, _HALF_K):
        lo = PEER_TOPK + (r - 1) * _HALF_K
        counts.append(jnp.sum(sel[lo:lo + _HALF_K], axis=0, keepdims=True))
    tail = sel[PEER_TOPK + (_HALF_K - 1) * _HALF_K:]
    return counts + [tail[r:r + 1] for r in range(_HALF_K)]


def _emit(k1_ref, k2_ref, h, count, rank2, p1, p2):
    k1_ref[h, 0] = 1.0 - count
    k1_ref[h, 1] = p1
    k2_ref[h, 0] = (-rank2).astype(k2_ref.dtype)
    k2_ref[h, 1] = p2.astype(k2_ref.dtype)


def _peer_topk_kernel(s_ref, k1_ref, k2_ref):
    n_heads = s_ref.shape[0] // 2
    group = min(TOPK_GROUP, n_heads)
    for h0 in range(0, n_heads, group):
        heads = list(range(h0, h0 + group))
        for h, bad in zip(heads, _peer_topk_fast(s_ref, k1_ref, k2_ref, heads)):
            @pl.when(jnp.max(jnp.where(bad, 1.0, 0.0)) > 0.0)
            def _(h=h):
                _peer_topk_exact(s_ref, k1_ref, k2_ref, h)


def _peer_topk_exact(s_ref, k1_ref, k2_ref, h):
    tb = s_ref.shape[2]
    iota = lax.broadcasted_iota(jnp.int32, (N_KEYS, tb), 0)
    i16 = lax.broadcasted_iota(jnp.int32, (PEER_TOPK, tb), 0)
    i8 = lax.broadcasted_iota(jnp.int32, (_HALF_K, tb), 0)
    flat = jnp.concatenate([i16] + [i8 + PEER_TOPK * r for r in range(1, _HALF_K)]
                           + [(i8 + _HALF_K) * PEER_TOPK], axis=0)
    s1 = s_ref[2 * h]
    s2 = s_ref[2 * h + 1]
    a, rank1 = _top16(s1, iota)
    b, rank2 = _top16(s2, iota)
    cand = _candidates(a, b)
    cmax = a[0] + b[0]
    sel = jnp.zeros(flat.shape, jnp.bool_)
    c = cand
    for _ in range(PEER_TOPK):
        m = jnp.max(c, axis=0, keepdims=True)
        idx = jnp.min(jnp.where(c == m, flat, PEER_TOPK * PEER_TOPK), axis=0, keepdims=True)
        hit = flat == idx
        sel = jnp.logical_or(sel, hit)
        c = jnp.where(hit, NEG_INF, c)
    self = sel.astype(F32)
    z = jnp.sum(self * jnp.exp(cand - cmax), axis=0, keepdims=True)
    lim = jnp.zeros((N_KEYS, tb), F32)
    for r, count in enumerate(_rank1_counts(self)):
        lim = jnp.where(rank1 == r, count, lim)
    _emit(k1_ref, k2_ref, h, lim, rank2.astype(F32), jnp.exp(s1 - a[0]) * (1.0 / z), jnp.exp(s2 - b[0]))


def peer_topk(scores, *, tb=256):
    c2, nk, t = scores.shape
    tb = min(tb, t)
    hh = c2 // 2
    return pl.pallas_call(
        _peer_topk_kernel,
        grid=(t // tb,),
        in_specs=[pl.BlockSpec((c2, nk, tb), lambda i: (0, 0, i))],
        out_specs=[pl.BlockSpec((hh, 2, nk, tb), lambda i: (0, 0, 0, i)),
                   pl.BlockSpec((hh, 2, nk, tb), lambda i: (0, 0, 0, i))],
        out_shape=[jax.ShapeDtypeStruct((hh, 2, nk, t), F32),
                   jax.ShapeDtypeStruct((hh, 2, nk, t), BF16)],
        compiler_params=_params("parallel"),
        name="peer_topk",
    )(scores)


def _gelu(x):
    return 0.5 * x * (1.0 + lax.erf(x * (1.0 / math.sqrt(2.0))))


def _peer_dense_kernel(h_ref, u_ref, v_ref, k1_ref, k2_ref, o_ref):
    j = pl.program_id(1)
    n1 = u_ref.shape[0] // N_KEYS
    tb = h_ref.shape[0]
    nt = (((1,), (1,)), ((), ()))
    tn = (((0,), (0,)), ((), ()))
    odd = (j % 2) == 1

    @pl.when(j == 0)
    def _():
        o_ref[...] = jnp.zeros_like(o_ref)

    a_t = lax.dot_general(u_ref[...], h_ref[...], nt, preferred_element_type=F32)
    parts = []
    for c in range(n1):
        g = jnp.zeros((N_KEYS, tb), BF16)
        for h in range(k1_ref.shape[0]):
            lo, hi = k1_ref[h, :, c:c + 1, :], k1_ref[h, :, n1 + c:n1 + c + 1, :]
            thr = jnp.broadcast_to(jnp.where(odd, hi[0], lo[0]), (N_KEYS, tb)).astype(BF16)
            p1 = jnp.broadcast_to(jnp.where(odd, hi[1], lo[1]), (N_KEYS, tb)).astype(BF16)
            g = g + jnp.where(k2_ref[h, 0] >= thr, k2_ref[h, 1] * p1, jnp.zeros((), BF16))
        parts.append(g)
    gate = jnp.concatenate(parts, axis=0)
    w_t = gate * _gelu(a_t).astype(BF16)
    o_ref[...] += lax.dot_general(w_t, v_ref[...], tn, preferred_element_type=F32)


def peer_dense(h, u, v, key1, key2, *, tb=512, ec=512, side=()):
    t, d = h.shape
    e = u.shape[0]
    tb = min(tb, t)
    hh, _, nk, _ = key1.shape
    assert 2 * (ec // nk) == SUBLANES
    nj = e // ec
    s_arr, s_in, s_out, s_shape = _side_specs(side, (t // tb) * nj, lambda i, j: i * nj + j)
    out = pl.pallas_call(
        _with_side_casts(_peer_dense_kernel, 5, len(side)),
        grid=(t // tb, nj),
        in_specs=[pl.BlockSpec((tb, d), lambda i, j: (i, 0)),
                  pl.BlockSpec((ec, d), lambda i, j: (j, 0)),
                  pl.BlockSpec((ec, d), lambda i, j: (j, 0)),
                  pl.BlockSpec((hh, 2, SUBLANES, tb), lambda i, j: (0, 0, j // 2, i)),
                  pl.BlockSpec((hh, 2, nk, tb), lambda i, j: (0, 0, 0, i))] + s_in,
        out_specs=[pl.BlockSpec((tb, d), lambda i, j: (i, 0))] + s_out,
        out_shape=[jax.ShapeDtypeStruct((t, d), F32)] + s_shape,
        compiler_params=_params("parallel", "arbitrary"),
        name="peer_dense",
    )(h, u, v, key1, key2, *s_arr)
    return tuple(out) if side else out[0]


def peer(h, wq, kk, u, v, side=()):
    scores = peer_scores(h, wq, kk)
    key1, key2 = peer_topk(scores)
    return peer_dense(h, u, v, key1, key2, side=side)


def _rope_tables(n):
    rows = n // GRID_W
    row_id = np.repeat(np.arange(rows), GRID_W).astype(np.float64)
    col_id = np.tile(np.arange(GRID_W), rows).astype(np.float64)
    axis_dim = QK_ROPE // 2
    inv_freq = ROPE_THETA ** (-np.arange(0, axis_dim, 2, dtype=np.float64) / axis_dim)
    ang = np.stack([row_id[:, None] * inv_freq, col_id[:, None] * inv_freq], axis=1)
    cos = np.cos(ang)
    sin = np.sin(ang)
    cos_full = np.concatenate([cos, cos], axis=-1).reshape(n, QK_ROPE)
    sin_full = np.concatenate([sin, sin], axis=-1).reshape(n, QK_ROPE)
    return np.concatenate([cos_full, sin_full], axis=-1).astype(np.float32)


def _rot_partner(w):
    ws = w.reshape(w.shape[:-1] + (2, 2, QK_ROPE // 4))
    return jnp.stack([-ws[..., 1, :], ws[..., 0, :]], axis=-2).reshape(w.shape)


def kernel(x, c, ctx, c_ctx, ada_w, ada_b, norm1_g, norm2_g, final_g, fnet_wo,
           mla_w_in, mla_q_g, mla_w_uq, mla_kv_g, mla_w_ukv, mla_wo,
           peer_wq, peer_k1, peer_k2, peer_u, peer_v):
    b, n, d = x.shape
    n_ctx = ctx.shape[1]
    depth = ada_w.shape[0]
    gd = d // F_GROUPS

    xl = x.reshape(b * n, d)
    xc = ctx.reshape(b * n_ctx, d)

    ct = jnp.zeros((d, SUBLANES), F32).at[:, :b].set(c.T).at[:, b].set(c_ctx)

    def sel_lat(i, bm):
        return (i * bm) // n

    def sel_ctx(i, bm):
        return b

    cc, sc = _dft_mats(gd)
    wcs = jnp.asarray(np.concatenate([cc, sc], axis=1)).astype(BF16)
    me_l, mo_l = (jnp.asarray(m).astype(BF16) for m in _pos_dft_mats(n))
    me_c, mo_c = (jnp.asarray(m).astype(BF16) for m in _pos_dft_mats(n_ctx))
    cs_lat = jnp.asarray(np.tile(_rope_tables(n), (b, 1)))
    cs_ctx = jnp.asarray(np.tile(np.concatenate([np.ones((1, QK_ROPE), np.float32),
                                                 np.zeros((1, QK_ROPE), np.float32)], axis=1),
                                 (b * n_ctx, 1)))
    q_scale = float((QK_NOPE + QK_ROPE) ** -0.5 * math.log2(math.e))

    pend_l = None
    pend_c = None
    uv = None
    for i in range(depth):
        last = i == depth - 1
        mod = ada_linear(ct, ada_w, ada_b, i, rows=b + 1)
        mod = mod.reshape(SUBLANES, ADA_CHUNKS, 1, d)

        fourier = i % 2 == 0
        if pend_l is None:
            hl = norm_block(xl, norm1_g[i], sel_lat, mod=(mod, (0, 1)), pair_rows=fourier)
            hc = norm_block(xc, norm1_g[i], sel_ctx, mod=(mod, (0, 1)), pair_rows=fourier)
        else:
            xl, hl = norm_block(xl, norm1_g[i], sel_lat, res=pend_l, mod=(mod, (0, 1)), write_x=True)
            xc, hc = norm_block(xc, norm1_g[i], sel_ctx, res=pend_c, mod=(mod, (0, 1)), write_x=True)
            if fourier:
                hl, hc = (h.reshape(h.shape[0] // 2, 2 * d) for h in (hl, hc))
        j = i // 2
        if fourier:
            wo = cast_layer(fnet_wo, j)
            if uv is None:
                fl, u_bf = pos_dft(me_l, mo_l, chan_dft(hl, wcs), b, side=[(peer_u, i)])
                yl, v_bf = matmul(fl, wo, side=[(peer_v, i)])
                uv = (u_bf, v_bf)
            else:
                fl = pos_dft(me_l, mo_l, chan_dft(hl, wcs), b)
                yl = matmul(fl, wo)
            if not last:
                fc = pos_dft(me_c, mo_c, chan_dft(hc, wcs), b)
                yc = matmul(fc, wo)
        else:
            w_in = mla_w_in[j]
            w_in = jnp.concatenate([w_in, _rot_partner(w_in[:, Q_LORA + KV_LORA:])], axis=1).astype(BF16)
            wq3 = mla_w_uq[j].reshape(Q_LORA, MLA_HEADS, QK_NOPE + QK_ROPE)
            wq_ext = jnp.concatenate([wq3, _rot_partner(wq3[..., QK_NOPE:])], axis=-1)
            wq_ext = wq_ext.reshape(Q_LORA, MLA_HEADS * (QK_NOPE + 2 * QK_ROPE)).astype(BF16)
            w_kv = cast_layer(mla_w_ukv, j)
            wo = cast_layer(mla_wo, j)
            zl = matmul(hl, w_in, out_dtype=F32, bn=w_in.shape[1])
            zc = matmul(hc, w_in, out_dtype=F32, bn=w_in.shape[1])
            q = q_proj(zl, mla_q_g[j] * q_scale, wq_ext, cs_lat)
            kv_l, kpe_l = kv_proj(zl, mla_kv_g[j], w_kv, cs_lat)
            kv_c, kpe_c = kv_proj(zc, mla_kv_g[j], w_kv, cs_ctx)
            o = mla_attention(q, kv_c, kpe_c, kv_l, kpe_l, batch=b, heads=MLA_HEADS)
            yl = matmul(o, wo)
            if not last:
                raise NotImplementedError("context output of an MLA layer that is not last")
        wq = cast_layer(peer_wq, i)
        kk = jnp.stack([peer_k1[i], peer_k2[i]], axis=1).reshape(2 * PEER_HEADS, N_KEYS, PEER_HALF)
        u, v = uv if uv is not None else (cast_layer(peer_u, i), cast_layer(peer_v, i))
        xl, h2l = norm_block(xl, norm2_g[i], sel_lat, res=(yl, mod, 2), mod=(mod, (3, 4)), write_x=True)
        if last:
            fl = peer(h2l, wq, kk, u, v)
        else:
            fl, *uv = peer(h2l, wq, kk, u, v, side=[(peer_u, i + 1), (peer_v, i + 1)])
        pend_l = (fl, mod, 5)
        if not last:
            xc, h2c = norm_block(xc, norm2_g[i], sel_ctx, res=(yc, mod, 2), mod=(mod, (3, 4)), write_x=True)
            pend_c = (peer(h2c, wq, kk, u, v), mod, 5)

    out = norm_block(xl, final_g, sel_lat, res=pend_l, out_dtype=F32)
    return out.reshape(b, n, d)
```

```python
import functools
import math

import numpy as np
import jax
import jax.numpy as jnp
from jax import lax
from jax.experimental import pallas as pl
from jax.experimental.pallas import tpu as pltpu

F32 = jnp.float32
BF16 = jnp.bfloat16

EPS = 1e-6
ADA_CHUNKS = 6
F_GROUPS = 8
GRID_W = 64
MLA_HEADS = 32
QK_NOPE = 128
QK_ROPE = 64
V_DIM = 128
Q_LORA = 1024
KV_LORA = 512
ROPE_THETA = 10000.0
PEER_HEADS = 8
PEER_HALF = 128
N_KEYS = 128
PEER_TOPK = 16

LANES = 128
SUBLANES = 8
VMEM_LIMIT = 56 * 1024 * 1024
NEG_INF = float("-inf")
TOPK_GROUP = 4
NO_RANK = 255.0
MAX_EXCESS = 64.0


def _params(*sem):
    return pltpu.CompilerParams(dimension_semantics=sem, vmem_limit_bytes=VMEM_LIMIT)


def _ada_kernel(ct_ref, w_ref, b_ref, o_ref, sb_ref, *, kc, rows):
    d, bn = w_ref.shape
    nt = bn // LANES

    @pl.when(pl.program_id(0) == 0)
    def _():
        c = ct_ref[...]
        s = c * jax.nn.sigmoid(c)
        for r in range(rows):
            sb_ref[r] = jnp.broadcast_to(s[:, r:r + 1], (d, LANES))

    def body(i, accs):
        k0 = pl.multiple_of(i * kc, kc)
        out = list(accs)
        for t in range(nt):
            wv = w_ref[pl.ds(k0, kc), t * LANES:(t + 1) * LANES]
            for r in range(rows):
                p = wv * sb_ref[r, pl.ds(k0, kc), :]
                out[r * nt + t] = out[r * nt + t] + p.reshape(kc // SUBLANES, SUBLANES, LANES).sum(axis=0)
        return tuple(out)

    init = tuple(jnp.zeros((SUBLANES, LANES), F32) for _ in range(rows * nt))
    accs = lax.fori_loop(0, d // kc, body, init)
    o_ref[...] = jnp.zeros_like(o_ref)
    for r in range(rows):
        for t in range(nt):
            row = accs[r * nt + t].sum(axis=0, keepdims=True) + b_ref[:, t * LANES:(t + 1) * LANES]
            o_ref[r:r + 1, t * LANES:(t + 1) * LANES] = row


def ada_linear(ct, w, b, layer, *, rows, bn=1024, kc=256):
    nl, d, n = w.shape
    bn = min(bn, n)
    kc = min(kc, d)
    return pl.pallas_call(
        functools.partial(_ada_kernel, kc=kc, rows=rows),
        grid=(n // bn,),
        in_specs=[
            pl.BlockSpec((d, SUBLANES), lambda j: (0, 0)),
            pl.BlockSpec((None, d, bn), lambda j: (layer, 0, j)),
            pl.BlockSpec((None, 1, bn), lambda j: (layer, 0, j)),
        ],
        out_specs=pl.BlockSpec((SUBLANES, bn), lambda j: (0, j)),
        out_shape=jax.ShapeDtypeStruct((SUBLANES, n), F32),
        scratch_shapes=[pltpu.VMEM((rows, d, LANES), F32)],
        compiler_params=_params("arbitrary"),
        name="ada_linear",
    )(ct, w, b.reshape(nl, 1, n))


def _cast_kernel(w_ref, o_ref):
    o_ref[...] = w_ref[...].astype(o_ref.dtype)


def cast_layer(w, layer, *, bm=512):
    _, r, c = w.shape
    bm = min(bm, r)
    return pl.pallas_call(
        _cast_kernel,
        grid=(r // bm,),
        in_specs=[pl.BlockSpec((None, bm, c), lambda i: (layer, i, 0))],
        out_specs=pl.BlockSpec((bm, c), lambda i: (i, 0)),
        out_shape=jax.ShapeDtypeStruct((r, c), BF16),
        compiler_params=_params("parallel"),
        name="cast_layer",
    )(w)


def _norm_kernel(*refs, has_res, has_mod, write_x, pair_rows):
    it = iter(refs)
    x_ref = next(it)
    if has_res:
        y_ref, gate_ref = next(it), next(it)
    g_ref = next(it)
    if has_mod:
        shift_ref, scale_ref = next(it), next(it)
    if write_x:
        xo_ref = next(it)
    h_ref = next(it)

    def normed(x):
        y = x * lax.rsqrt(jnp.mean(x * x, axis=-1, keepdims=True) + EPS)
        y = y * g_ref[...]
        if has_mod:
            y = y * (1.0 + scale_ref[...]) + shift_ref[...]
        return y.astype(h_ref.dtype)

    if pair_rows:
        bm, d = x_ref.shape
        h = normed(x_ref[...])
        r = lax.broadcasted_iota(jnp.int32, (bm // 2, bm), 0)
        c = lax.broadcasted_iota(jnp.int32, (bm // 2, bm), 1)
        for p in range(2):
            pick = jnp.where(c == 2 * r + p, 1.0, 0.0).astype(h.dtype)
            h_ref[:, p * d:(p + 1) * d] = jnp.dot(pick, h, preferred_element_type=F32).astype(h_ref.dtype)
        return

    x = x_ref[...]
    if has_res:
        x = x + gate_ref[...] * y_ref[...].astype(F32)
    if write_x:
        xo_ref[...] = x
    h_ref[...] = normed(x)


def norm_block(x, g, sel, *, res=None, mod=None, write_x=False, pair_rows=False, out_dtype=BF16, bm=256):
    t, d = x.shape
    bm = min(bm, t)
    assert not (pair_rows and (res is not None or write_x))
    row = pl.BlockSpec((bm, d), lambda i: (i, 0))
    h_shape, h_block = ((t // 2, 2 * d), (bm // 2, 2 * d)) if pair_rows else ((t, d), (bm, d))

    def mod_spec(a):
        return pl.BlockSpec((None, None, 1, d), lambda i, a=a: (sel(i, bm), a, 0, 0))

    args, specs = [x], [row]
    if res is not None:
        args += [res[0], res[1]]
        specs += [row, mod_spec(res[2])]
    args.append(g.reshape(1, d))
    specs.append(pl.BlockSpec((1, d), lambda i: (0, 0)))
    if mod is not None:
        args += [mod[0], mod[0]]
        specs += [mod_spec(mod[1][0]), mod_spec(mod[1][1])]
    out_shape, out_specs = [], []
    if write_x:
        out_shape.append(jax.ShapeDtypeStruct((t, d), F32))
        out_specs.append(row)
    out_shape.append(jax.ShapeDtypeStruct(h_shape, out_dtype))
    out_specs.append(pl.BlockSpec(h_block, lambda i: (i, 0)))
    out = pl.pallas_call(
        functools.partial(_norm_kernel, has_res=res is not None, has_mod=mod is not None, write_x=write_x,
                          pair_rows=pair_rows),
        grid=(t // bm,),
        in_specs=specs,
        out_specs=out_specs,
        out_shape=out_shape,
        compiler_params=_params("parallel"),
        name="norm_block",
    )(*args)
    return out if write_x else out[0]


def _side_specs(side, n_steps, step_of):
    arrays, ins, outs, shapes = [], [], [], []
    for w, layer in side:
        _, r, c = w.shape
        rows = r // n_steps
        assert rows * n_steps == r and rows % (2 * SUBLANES) == 0
        arrays.append(w)
        ins.append(pl.BlockSpec((None, rows, c), lambda *g, layer=layer: (layer, step_of(*g), 0)))
        outs.append(pl.BlockSpec((rows, c), lambda *g: (step_of(*g), 0)))
        shapes.append(jax.ShapeDtypeStruct((r, c), BF16))
    return arrays, ins, outs, shapes


def _with_side_casts(body, n_in, n_side):
    def kernel(*refs):
        main_in, side_in = refs[:n_in], refs[n_in:n_in + n_side]
        rest = refs[n_in + n_side:]
        main_out, side_out = rest[:len(rest) - n_side], rest[len(rest) - n_side:]
        body(*main_in, *main_out)
        for src, dst in zip(side_in, side_out):
            dst[...] = src[...].astype(dst.dtype)
    return kernel


def _mm_kernel(a_ref, b_ref, o_ref):
    o_ref[...] = jnp.dot(a_ref[...], b_ref[...], preferred_element_type=F32).astype(o_ref.dtype)


def matmul(a, b, *, out_dtype=BF16, bm=512, bn=1024, side=()):
    m, k = a.shape
    _, n = b.shape
    bm, bn = min(bm, m), min(bn, n)
    ni = m // bm
    s_arr, s_in, s_out, s_shape = _side_specs(side, (n // bn) * ni, lambda j, i: j * ni + i)
    out = pl.pallas_call(
        _with_side_casts(_mm_kernel, 2, len(side)),
        grid=(n // bn, ni),
        in_specs=[pl.BlockSpec((bm, k), lambda j, i: (i, 0)),
                  pl.BlockSpec((k, bn), lambda j, i: (0, j))] + s_in,
        out_specs=[pl.BlockSpec((bm, bn), lambda j, i: (i, j))] + s_out,
        out_shape=[jax.ShapeDtypeStruct((m, n), out_dtype)] + s_shape,
        compiler_params=_params("parallel", "parallel"),
        name="matmul",
    )(a, b, *s_arr)
    return tuple(out) if side else out[0]


def _dft_mats(n):
    k = np.arange(n, dtype=np.int64)
    ang = 2.0 * np.pi * ((k[:, None] * k[None, :]) % n).astype(np.float64) / n
    s = 1.0 / math.sqrt(n)
    return (np.cos(ang) * s).astype(np.float32), (np.sin(ang) * s).astype(np.float32)


def _chan_dft_kernel(h_ref, w_ref, o_ref):
    gd = h_ref.shape[1]
    y = jnp.dot(h_ref[...], w_ref[...], preferred_element_type=F32)
    o_ref[0] = y[:, :gd].astype(o_ref.dtype)
    o_ref[1] = y[:, gd:].astype(o_ref.dtype)


def chan_dft(h, wcs, *, bm=1024):
    t, d = h.shape
    gd = wcs.shape[0]
    bm = min(bm, t)
    return pl.pallas_call(
        _chan_dft_kernel,
        grid=(t // bm, d // gd),
        in_specs=[pl.BlockSpec((bm, gd), lambda i, g: (i, g)),
                  pl.BlockSpec((gd, 2 * gd), lambda i, g: (0, 0))],
        out_specs=pl.BlockSpec((2, bm, gd), lambda i, g: (0, i, g)),
        out_shape=jax.ShapeDtypeStruct((2, t, d), BF16),
        compiler_params=_params("parallel", "parallel"),
        name="chan_dft",
    )(h, wcs)


def _pos_dft_mats(n):
    cn, sn = _dft_mats(n)
    half = n // 2
    me = np.concatenate([cn[:half, 0::2], -sn[:half, 0::2]], axis=1)
    mo = np.concatenate([cn[:half, 1::2], -sn[:half, 1::2]], axis=1)
    return me, mo


def _pos_dft_kernel(me_ref, mo_ref, yce_ref, yse_ref, yco_ref, yso_ref, o_ref):
    ye = jnp.concatenate([yce_ref[...], yse_ref[...]], axis=0)
    yo = jnp.concatenate([yco_ref[...], yso_ref[...]], axis=0)
    e = jnp.dot(me_ref[...], ye, preferred_element_type=F32)
    o = jnp.dot(mo_ref[...], yo, preferred_element_type=F32)
    o_ref[0] = (e + o).astype(o_ref.dtype)
    o_ref[1] = (e - o).astype(o_ref.dtype)


def pos_dft(me, mo, y2, n_seq, *, bm=512, bn=512, side=()):
    half, n = me.shape
    _, t2, d2 = y2.shape
    t, d = 2 * t2, d2 // 2
    bm, bn = min(bm, half), min(bn, d)
    nb = d // bn
    ni = half // bm
    s_arr, s_in, s_out, s_shape = _side_specs(side, n_seq * nb * ni, lambda b, j, i: (b * nb + j) * ni + i)
    out = pl.pallas_call(
        _with_side_casts(_pos_dft_kernel, 6, len(side)),
        grid=(n_seq, nb, ni),
        in_specs=[pl.BlockSpec((bm, n), lambda b, j, i: (i, 0)),
                  pl.BlockSpec((bm, n), lambda b, j, i: (i, 0)),
                  pl.BlockSpec((None, half, bn), lambda b, j, i: (0, b, j)),
                  pl.BlockSpec((None, half, bn), lambda b, j, i: (1, b, j)),
                  pl.BlockSpec((None, half, bn), lambda b, j, i: (0, b, nb + j)),
                  pl.BlockSpec((None, half, bn), lambda b, j, i: (1, b, nb + j))] + s_in,
        out_specs=[pl.BlockSpec((None, 2, bm, bn), lambda b, j, i: (b, 0, i, j))] + s_out,
        out_shape=[jax.ShapeDtypeStruct((n_seq, 2, half, d), BF16)] + s_shape,
        compiler_params=_params("parallel", "parallel", "parallel"),
        name="pos_dft",
    )(me, mo, y2, y2, y2, y2, *s_arr)
    f = out[0].reshape(t, d)
    return (f, *out[1:]) if side else f


def _rms_prologue(z, g):
    zn = z * lax.rsqrt(jnp.mean(z * z, axis=-1, keepdims=True) + EPS)
    return (zn * g).astype(BF16)


def _q_proj_kernel(z_ref, g_ref, w_ref, cs_ref, o_ref, zn_ref):
    @pl.when(pl.program_id(1) == 0)
    def _():
        zn_ref[...] = _rms_prologue(z_ref[:, :Q_LORA], g_ref[...])

    acc = jnp.dot(zn_ref[...], w_ref[...], preferred_element_type=F32)
    cs = cs_ref[...]
    hw = QK_NOPE + 2 * QK_ROPE
    for h in range(w_ref.shape[1] // hw):
        o_ref[:, h * hw:h * hw + QK_NOPE] = acc[:, h * hw:h * hw + QK_NOPE].astype(o_ref.dtype)
        t = acc[:, h * hw + QK_NOPE:(h + 1) * hw] * cs
        r = t + pltpu.roll(t, QK_ROPE, axis=1)
        o_ref[:, h * hw + QK_NOPE:(h + 1) * hw] = r.astype(o_ref.dtype)


def q_proj(z, q_g, w_q, cs_tab, *, bm=1024, bn=2048):
    t, zw = z.shape
    n = w_q.shape[1]
    bm = min(bm, t)
    return pl.pallas_call(
        _q_proj_kernel,
        grid=(t // bm, n // bn),
        in_specs=[pl.BlockSpec((bm, zw), lambda i, j: (i, 0)),
                  pl.BlockSpec((1, Q_LORA), lambda i, j: (0, 0)),
                  pl.BlockSpec((Q_LORA, bn), lambda i, j: (0, j)),
                  pl.BlockSpec((bm, 2 * QK_ROPE), lambda i, j: (i, 0))],
        out_specs=pl.BlockSpec((bm, bn), lambda i, j: (i, j)),
        out_shape=jax.ShapeDtypeStruct((t, n), BF16),
        scratch_shapes=[pltpu.VMEM((bm, Q_LORA), BF16)],
        compiler_params=_params("parallel", "arbitrary"),
        name="q_proj",
    )(z, q_g.reshape(1, Q_LORA), w_q, cs_tab)


def _kv_proj_kernel(z_ref, g_ref, w_ref, cs_ref, o_ref, kpe_ref, zn_ref):
    @pl.when(pl.program_id(1) == 0)
    def _():
        zn_ref[...] = _rms_prologue(z_ref[:, Q_LORA:Q_LORA + KV_LORA], g_ref[...])
        t = z_ref[:, Q_LORA + KV_LORA:] * cs_ref[...]
        r = t + pltpu.roll(t, QK_ROPE, axis=1)
        lane = lax.broadcasted_iota(jnp.int32, r.shape, 1)
        kpe_ref[...] = jnp.where(lane < QK_ROPE, r, 0.0).astype(kpe_ref.dtype)

    o_ref[...] = jnp.dot(zn_ref[...], w_ref[...], preferred_element_type=F32).astype(o_ref.dtype)


def kv_proj(z, kv_g, w_kv, cs_tab, *, bm=1024, bn=2048):
    t, zw = z.shape
    n = w_kv.shape[1]
    bm = min(bm, t)
    return pl.pallas_call(
        _kv_proj_kernel,
        grid=(t // bm, n // bn),
        in_specs=[pl.BlockSpec((bm, zw), lambda i, j: (i, 0)),
                  pl.BlockSpec((1, KV_LORA), lambda i, j: (0, 0)),
                  pl.BlockSpec((KV_LORA, bn), lambda i, j: (0, j)),
                  pl.BlockSpec((bm, 2 * QK_ROPE), lambda i, j: (i, 0))],
        out_specs=[pl.BlockSpec((bm, bn), lambda i, j: (i, j)),
                   pl.BlockSpec((bm, 2 * QK_ROPE), lambda i, j: (i, 0))],
        out_shape=[jax.ShapeDtypeStruct((t, n), BF16),
                   jax.ShapeDtypeStruct((t, 2 * QK_ROPE), BF16)],
        scratch_shapes=[pltpu.VMEM((bm, KV_LORA), BF16)],
        compiler_params=_params("parallel", "arbitrary"),
        name="kv_proj",
    )(z, kv_g.reshape(1, KV_LORA), w_kv, cs_tab)


def _attn_kernel(q_ref, kc_ref, pc_ref, vc_ref, kl_ref, pl_ref, vl_ref, o_ref, *, tq, tk):
    n = q_ref.shape[0]
    n_lat = kl_ref.shape[0]
    nt = (((1,), (1,)), ((), ()))

    def key_chunks():
        yield (jnp.concatenate([kc_ref[...], pc_ref[...]], axis=1), vc_ref[...])
        for c in range(n_lat // tk):
            rows = slice(c * tk, (c + 1) * tk)
            yield (jnp.concatenate([kl_ref[rows, :], pl_ref[rows, :]], axis=1), vl_ref[rows, :])

    def lane_max(s):
        m = s[:, :LANES]
        for j in range(1, s.shape[1] // LANES):
            m = jnp.maximum(m, s[:, j * LANES:(j + 1) * LANES])
        return m

    def one_pass(q):
        ref = None
        acc = None
        top = None
        for k, v in key_chunks():
            s = lax.dot_general(q, k, nt, preferred_element_type=F32)
            if ref is None:
                ref = jnp.max(s, axis=-1, keepdims=True)
            p = jnp.exp2(s - ref).astype(BF16)
            v1 = jnp.concatenate([v, jnp.ones_like(v)], axis=1)
            part = jnp.dot(p, v1, preferred_element_type=F32)
            acc = part if acc is None else acc + part
            m = lane_max(s)
            top = m if top is None else jnp.maximum(top, m)
        return acc[:, :V_DIM] / acc[:, V_DIM:], jnp.max(top - ref)

    def online(q):
        m = jnp.full((tq, 1), NEG_INF, F32)
        l = jnp.zeros((tq, 1), F32)
        acc = jnp.zeros((tq, V_DIM), F32)
        for k, v in key_chunks():
            s = lax.dot_general(q, k, nt, preferred_element_type=F32)
            m_new = jnp.maximum(m, jnp.max(s, axis=-1, keepdims=True))
            alpha = jnp.exp2(m - m_new)
            p = jnp.exp2(s - m_new)
            l = alpha * l + jnp.sum(p, axis=-1, keepdims=True)
            acc = alpha * acc + jnp.dot(p.astype(BF16), v, preferred_element_type=F32)
            m = m_new
        return acc / l

    def q_tile(i, carry):
        r0 = pl.multiple_of(i * tq, tq)
        q = q_ref[pl.ds(r0, tq), :]
        out, excess = one_pass(q)
        o_ref[pl.ds(r0, tq), :] = out.astype(o_ref.dtype)

        @pl.when(jnp.logical_not(excess <= MAX_EXCESS))
        def _():
            o_ref[pl.ds(r0, tq), :] = online(q).astype(o_ref.dtype)

        return carry

    lax.fori_loop(0, n // tq, q_tile, 0)


def mla_attention(q, kv_c, kpe_c, kv_l, kpe_l, *, batch, heads, tq=1024, tk=512):
    n = q.shape[0] // batch
    n_ctx = kv_c.shape[0] // batch
    tq, tk = min(tq, n), min(tk, n)
    hq = QK_NOPE + 2 * QK_ROPE
    return pl.pallas_call(
        functools.partial(_attn_kernel, tq=tq, tk=tk),
        grid=(batch, heads),
        in_specs=[pl.BlockSpec((n, hq), lambda b, h: (b, h)),
                  pl.BlockSpec((n_ctx, QK_NOPE), lambda b, h: (b, 2 * h)),
                  pl.BlockSpec((n_ctx, 2 * QK_ROPE), lambda b, h: (b, 0)),
                  pl.BlockSpec((n_ctx, V_DIM), lambda b, h: (b, 2 * h + 1)),
                  pl.BlockSpec((n, QK_NOPE), lambda b, h: (b, 2 * h)),
                  pl.BlockSpec((n, 2 * QK_ROPE), lambda b, h: (b, 0)),
                  pl.BlockSpec((n, V_DIM), lambda b, h: (b, 2 * h + 1))],
        out_specs=pl.BlockSpec((n, V_DIM), lambda b, h: (b, h)),
        out_shape=jax.ShapeDtypeStruct((batch * n, heads * V_DIM), BF16),
        compiler_params=_params("parallel", "parallel"),
        name="mla_attention",
    )(q, kv_c, kpe_c, kv_c, kv_l, kpe_l, kv_l)


def _peer_scores_kernel(h_ref, wq_ref, kk_ref, o_ref):
    q = jnp.dot(h_ref[...], wq_ref[...], preferred_element_type=F32)
    nt = (((1,), (1,)), ((), ()))
    q_hi = q.astype(BF16)
    q_lo = (q - q_hi.astype(F32)).astype(BF16)
    for c in range(wq_ref.shape[1] // PEER_HALF):
        cols = slice(c * PEER_HALF, (c + 1) * PEER_HALF)
        k = kk_ref[c]
        k_hi = k.astype(BF16)
        k_lo = (k - k_hi.astype(F32)).astype(BF16)
        s = lax.dot_general(k_hi, q_hi[:, cols], nt, preferred_element_type=F32)
        s = s + lax.dot_general(k_lo, q_hi[:, cols], nt, preferred_element_type=F32)
        s = s + lax.dot_general(k_hi, q_lo[:, cols], nt, preferred_element_type=F32)
        o_ref[c] = s


def peer_scores(h, wq, kk, *, bm=512, bn=1024):
    t, d = h.shape
    n = wq.shape[1]
    bm, bn = min(bm, t), min(bn, n)
    cb = bn // PEER_HALF
    return pl.pallas_call(
        _peer_scores_kernel,
        grid=(n // bn, t // bm),
        in_specs=[pl.BlockSpec((bm, d), lambda j, i: (i, 0)),
                  pl.BlockSpec((d, bn), lambda j, i: (0, j)),
                  pl.BlockSpec((cb, N_KEYS, PEER_HALF), lambda j, i: (j, 0, 0))],
        out_specs=pl.BlockSpec((cb, N_KEYS, bm), lambda j, i: (j, 0, i)),
        out_shape=jax.ShapeDtypeStruct((n // PEER_HALF, N_KEYS, t), F32),
        compiler_params=_params("parallel", "parallel"),
        name="peer_scores",
    )(h, wq, kk)


_HALF_K = PEER_TOPK // 2


def _top16(s, iota):
    vals = []
    rank = jnp.full(s.shape, int(NO_RANK), jnp.int32)
    for r in range(PEER_TOPK):
        m = jnp.max(s, axis=0, keepdims=True)
        idx = jnp.min(jnp.where(s == m, iota, N_KEYS), axis=0, keepdims=True)
        hit = iota == idx
        rank = jnp.where(hit, r, rank)
        s = jnp.where(hit, NEG_INF, s)
        vals.append(m)
    return vals, rank


def _extract16(arrays, with_rank):
    arrays = list(arrays)
    vals = [[] for _ in arrays]
    ranks = [jnp.full(s.shape, NO_RANK, F32) if w else None for s, w in zip(arrays, with_rank)]
    for r in range(PEER_TOPK):
        for i, s in enumerate(arrays):
            m = jnp.max(s, axis=0, keepdims=True)
            vals[i].append(m)
            hit = s == m
            if with_rank[i]:
                ranks[i] = jnp.where(hit, float(r), ranks[i])
            arrays[i] = jnp.where(hit, NEG_INF, s)
    bad = [jnp.sum(jnp.where(s == NEG_INF, 1.0, 0.0), axis=0, keepdims=True) != float(PEER_TOPK)
           for s in arrays]
    return vals, bad, ranks


def _candidates(a, b):
    a_hi = jnp.concatenate(a[_HALF_K:], axis=0)
    b_all = jnp.concatenate(b, axis=0)
    b_lo = b_all[:_HALF_K]
    return jnp.concatenate([a[0] + b_all] + [a[r] + b_lo for r in range(1, _HALF_K)]
                           + [a_hi + b[0]], axis=0)


def _peer_topk_fast(s_ref, k1_ref, k2_ref, heads):
    tb = s_ref.shape[2]
    s = [s_ref[2 * h + i] for h in heads for i in range(2)]
    tops, bad_s, ranks = _extract16(s, [False, True] * len(heads))
    cands = [_candidates(tops[2 * n], tops[2 * n + 1]) for n in range(len(heads))]
    sums, bad_c, _ = _extract16(cands, [False] * len(heads))
    for n, h in enumerate(heads):
        s1, s2, a, b, cand = s[2 * n], s[2 * n + 1], tops[2 * n], tops[2 * n + 1], cands[n]
        keep = jnp.where(cand >= sums[n][-1], 1.0, 0.0)
        z = jnp.sum(keep * jnp.exp(cand - sums[n][0]), axis=0, keepdims=True)
        lim = jnp.zeros((N_KEYS, tb), F32)
        for r, count in enumerate(_rank1_counts(keep)):
            lim = jnp.where(s1 == a[r], count, lim)
        _emit(k1_ref, k2_ref, h, lim, ranks[2 * n + 1], jnp.exp(s1 - a[0]) * (1.0 / z), jnp.exp(s2 - b[0]))
    return [bad_s[2 * n] | bad_s[2 * n + 1] | bad_c[n] for n in range(len(heads))]


def _rank1_counts(sel):
    counts = [jnp.sum(sel[:PEER_TOPK], axis=0, keepdims=True)]
    for r in range(1, _HALF_K):
        lo = PEER_TOPK + (r - 1) * _HALF_K
        counts.append(jnp.sum(sel[lo:lo + _HALF_K], axis=0, keepdims=True))
    tail = sel[PEER_TOPK + (_HALF_K - 1) * _HALF_K:]
    return counts + [tail[r:r + 1] for r in range(_HALF_K)]


def _emit(k1_ref, k2_ref, h, count, rank2, p1, p2):
    k1_ref[h, 0] = 1.0 - count
    k1_ref[h, 1] = 0.5 * p1
    k2_ref[h, 0] = (-rank2).astype(k2_ref.dtype)
    k2_ref[h, 1] = p2.astype(k2_ref.dtype)


def _peer_topk_kernel(s_ref, k1_ref, k2_ref):
    n_heads = s_ref.shape[0] // 2
    group = min(TOPK_GROUP, n_heads)
    for h0 in range(0, n_heads, group):
        heads = list(range(h0, h0 + group))
        for h, bad in zip(heads, _peer_topk_fast(s_ref, k1_ref, k2_ref, heads)):
            @pl.when(jnp.max(jnp.where(bad, 1.0, 0.0)) > 0.0)
            def _(h=h):
                _peer_topk_exact(s_ref, k1_ref, k2_ref, h)


def _peer_topk_exact(s_ref, k1_ref, k2_ref, h):
    tb = s_ref.shape[2]
    iota = lax.broadcasted_iota(jnp.int32, (N_KEYS, tb), 0)
    i16 = lax.broadcasted_iota(jnp.int32, (PEER_TOPK, tb), 0)
    i8 = lax.broadcasted_iota(jnp.int32, (_HALF_K, tb), 0)
    flat = jnp.concatenate([i16] + [i8 + PEER_TOPK * r for r in range(1, _HALF_K)]
                           + [(i8 + _HALF_K) * PEER_TOPK], axis=0)
    s1 = s_ref[2 * h]
    s2 = s_ref[2 * h + 1]
    a, rank1 = _top16(s1, iota)
    b, rank2 = _top16(s2, iota)
    cand = _candidates(a, b)
    cmax = a[0] + b[0]
    sel = jnp.zeros(flat.shape, jnp.bool_)
    c = cand
    for _ in range(PEER_TOPK):
        m = jnp.max(c, axis=0, keepdims=True)
        idx = jnp.min(jnp.where(c == m, flat, PEER_TOPK * PEER_TOPK), axis=0, keepdims=True)
        hit = flat == idx
        sel = jnp.logical_or(sel, hit)
        c = jnp.where(hit, NEG_INF, c)
    self = sel.astype(F32)
    z = jnp.sum(self * jnp.exp(cand - cmax), axis=0, keepdims=True)
    lim = jnp.zeros((N_KEYS, tb), F32)
    for r, count in enumerate(_rank1_counts(self)):
        lim = jnp.where(rank1 == r, count, lim)
    _emit(k1_ref, k2_ref, h, lim, rank2.astype(F32), jnp.exp(s1 - a[0]) * (1.0 / z), jnp.exp(s2 - b[0]))


def peer_topk(scores, *, tb=256):
    c2, nk, t = scores.shape
    tb = min(tb, t)
    hh = c2 // 2
    return pl.pallas_call(
        _peer_topk_kernel,
        grid=(t // tb,),
        in_specs=[pl.BlockSpec((c2, nk, tb), lambda i: (0, 0, i))],
        out_specs=[pl.BlockSpec((hh, 2, nk, tb), lambda i: (0, 0, 0, i)),
                   pl.BlockSpec((hh, 2, nk, tb), lambda i: (0, 0, 0, i))],
        out_shape=[jax.ShapeDtypeStruct((hh, 2, nk, t), F32),
                   jax.ShapeDtypeStruct((hh, 2, nk, t), BF16)],
        compiler_params=_params("parallel"),
        name="peer_topk",
    )(scores)


def _gelu_x2(x):
    return x + x * lax.erf(x * (1.0 / math.sqrt(2.0)))


def _peer_dense_kernel(h_ref, u_ref, v_ref, k1_ref, k2_ref, o_ref):
    j = pl.program_id(1)
    n1 = u_ref.shape[0] // N_KEYS
    tb = h_ref.shape[0]
    nt = (((1,), (1,)), ((), ()))
    tn = (((0,), (0,)), ((), ()))
    odd = (j % 2) == 1

    @pl.when(j == 0)
    def _():
        o_ref[...] = jnp.zeros_like(o_ref)

    a_t = lax.dot_general(u_ref[...], h_ref[...], nt, preferred_element_type=F32)
    parts = []
    for c in range(n1):
        g = jnp.zeros((N_KEYS, tb), BF16)
        for h in range(k1_ref.shape[0]):
            lo, hi = k1_ref[h, :, c:c + 1, :], k1_ref[h, :, n1 + c:n1 + c + 1, :]
            thr = jnp.broadcast_to(jnp.where(odd, hi[0], lo[0]), (N_KEYS, tb)).astype(BF16)
            p1 = jnp.broadcast_to(jnp.where(odd, hi[1], lo[1]), (N_KEYS, tb)).astype(BF16)
            g = g + jnp.where(k2_ref[h, 0] >= thr, k2_ref[h, 1] * p1, jnp.zeros((), BF16))
        parts.append(g)
    gate = jnp.concatenate(parts, axis=0)
    w_t = gate * _gelu_x2(a_t).astype(BF16)
    o_ref[...] += lax.dot_general(w_t, v_ref[...], tn, preferred_element_type=F32)


def peer_dense(h, u, v, key1, key2, *, tb=512, ec=512, side=()):
    t, d = h.shape
    e = u.shape[0]
    tb = min(tb, t)
    hh, _, nk, _ = key1.shape
    assert 2 * (ec // nk) == SUBLANES
    nj = e // ec
    s_arr, s_in, s_out, s_shape = _side_specs(side, (t // tb) * nj, lambda i, j: i * nj + j)
    out = pl.pallas_call(
        _with_side_casts(_peer_dense_kernel, 5, len(side)),
        grid=(t // tb, nj),
        in_specs=[pl.BlockSpec((tb, d), lambda i, j: (i, 0)),
                  pl.BlockSpec((ec, d), lambda i, j: (j, 0)),
                  pl.BlockSpec((ec, d), lambda i, j: (j, 0)),
                  pl.BlockSpec((hh, 2, SUBLANES, tb), lambda i, j: (0, 0, j // 2, i)),
                  pl.BlockSpec((hh, 2, nk, tb), lambda i, j: (0, 0, 0, i))] + s_in,
        out_specs=[pl.BlockSpec((tb, d), lambda i, j: (i, 0))] + s_out,
        out_shape=[jax.ShapeDtypeStruct((t, d), F32)] + s_shape,
        compiler_params=_params("parallel", "arbitrary"),
        name="peer_dense",
    )(h, u, v, key1, key2, *s_arr)
    return tuple(out) if side else out[0]


def peer(h, wq, kk, u, v, side=()):
    scores = peer_scores(h, wq, kk)
    key1, key2 = peer_topk(scores)
    return peer_dense(h, u, v, key1, key2, side=side)


def _rope_tables(n):
    rows = n // GRID_W
    row_id = np.repeat(np.arange(rows), GRID_W).astype(np.float64)
    col_id = np.tile(np.arange(GRID_W), rows).astype(np.float64)
    axis_dim = QK_ROPE // 2
    inv_freq = ROPE_THETA ** (-np.arange(0, axis_dim, 2, dtype=np.float64) / axis_dim)
    ang = np.stack([row_id[:, None] * inv_freq, col_id[:, None] * inv_freq], axis=1)
    cos = np.cos(ang)
    sin = np.sin(ang)
    cos_full = np.concatenate([cos, cos], axis=-1).reshape(n, QK_ROPE)
    sin_full = np.concatenate([sin, sin], axis=-1).reshape(n, QK_ROPE)
    return np.concatenate([cos_full, sin_full], axis=-1).astype(np.float32)


def _rot_partner(w):
    ws = w.reshape(w.shape[:-1] + (2, 2, QK_ROPE // 4))
    return jnp.stack([-ws[..., 1, :], ws[..., 0, :]], axis=-2).reshape(w.shape)


def kernel(x, c, ctx, c_ctx, ada_w, ada_b, norm1_g, norm2_g, final_g, fnet_wo,
           mla_w_in, mla_q_g, mla_w_uq, mla_kv_g, mla_w_ukv, mla_wo,
           peer_wq, peer_k1, peer_k2, peer_u, peer_v):
    b, n, d = x.shape
    n_ctx = ctx.shape[1]
    depth = ada_w.shape[0]
    gd = d // F_GROUPS

    xl = x.reshape(b * n, d)
    xc = ctx.reshape(b * n_ctx, d)

    ct = jnp.zeros((d, SUBLANES), F32).at[:, :b].set(c.T).at[:, b].set(c_ctx)

    def sel_lat(i, bm):
        return (i * bm) // n

    def sel_ctx(i, bm):
        return b

    cc, sc = _dft_mats(gd)
    wcs = jnp.asarray(np.concatenate([cc, sc], axis=1)).astype(BF16)
    me_l, mo_l = (jnp.asarray(m).astype(BF16) for m in _pos_dft_mats(n))
    me_c, mo_c = (jnp.asarray(m).astype(BF16) for m in _pos_dft_mats(n_ctx))
    cs_lat = jnp.asarray(np.tile(_rope_tables(n), (b, 1)))
    cs_ctx = jnp.asarray(np.tile(np.concatenate([np.ones((1, QK_ROPE), np.float32),
                                                 np.zeros((1, QK_ROPE), np.float32)], axis=1),
                                 (b * n_ctx, 1)))
    q_scale = float((QK_NOPE + QK_ROPE) ** -0.5 * math.log2(math.e))

    pend_l = None
    pend_c = None
    uv = None
    for i in range(depth):
        last = i == depth - 1
        mod = ada_linear(ct, ada_w, ada_b, i, rows=b + 1)
        mod = mod.reshape(SUBLANES, ADA_CHUNKS, 1, d)

        fourier = i % 2 == 0
        if pend_l is None:
            hl = norm_block(xl, norm1_g[i], sel_lat, mod=(mod, (0, 1)), pair_rows=fourier)
            hc = norm_block(xc, norm1_g[i], sel_ctx, mod=(mod, (0, 1)), pair_rows=fourier)
        else:
            xl, hl = norm_block(xl, norm1_g[i], sel_lat, res=pend_l, mod=(mod, (0, 1)), write_x=True)
            xc, hc = norm_block(xc, norm1_g[i], sel_ctx, res=pend_c, mod=(mod, (0, 1)), write_x=True)
            if fourier:
                hl, hc = (h.reshape(h.shape[0] // 2, 2 * d) for h in (hl, hc))
        j = i // 2
        if fourier:
            wo = cast_layer(fnet_wo, j)
            if uv is None:
                fl, u_bf = pos_dft(me_l, mo_l, chan_dft(hl, wcs), b, side=[(peer_u, i)])
                yl, v_bf = matmul(fl, wo, side=[(peer_v, i)])
                uv = (u_bf, v_bf)
            else:
                fl = pos_dft(me_l, mo_l, chan_dft(hl, wcs), b)
                yl = matmul(fl, wo)
            if not last:
                fc = pos_dft(me_c, mo_c, chan_dft(hc, wcs), b)
                yc = matmul(fc, wo)
        else:
            w_in = mla_w_in[j]
            w_in = jnp.concatenate([w_in, _rot_partner(w_in[:, Q_LORA + KV_LORA:])], axis=1).astype(BF16)
            wq3 = mla_w_uq[j].reshape(Q_LORA, MLA_HEADS, QK_NOPE + QK_ROPE)
            wq_ext = jnp.concatenate([wq3, _rot_partner(wq3[..., QK_NOPE:])], axis=-1)
            wq_ext = wq_ext.reshape(Q_LORA, MLA_HEADS * (QK_NOPE + 2 * QK_ROPE)).astype(BF16)
            w_kv = cast_layer(mla_w_ukv, j)
            wo = cast_layer(mla_wo, j)
            zl = matmul(hl, w_in, out_dtype=F32, bn=w_in.shape[1])
            zc = matmul(hc, w_in, out_dtype=F32, bn=w_in.shape[1])
            q = q_proj(zl, mla_q_g[j] * q_scale, wq_ext, cs_lat)
            kv_l, kpe_l = kv_proj(zl, mla_kv_g[j], w_kv, cs_lat)
            kv_c, kpe_c = kv_proj(zc, mla_kv_g[j], w_kv, cs_ctx)
            o = mla_attention(q, kv_c, kpe_c, kv_l, kpe_l, batch=b, heads=MLA_HEADS)
            yl = matmul(o, wo)
            if not last:
                raise NotImplementedError("context output of an MLA layer that is not last")
        wq = cast_layer(peer_wq, i)
        kk = jnp.stack([peer_k1[i], peer_k2[i]], axis=1).reshape(2 * PEER_HEADS, N_KEYS, PEER_HALF)
        u, v = uv if uv is not None else (cast_layer(peer_u, i), cast_layer(peer_v, i))
        xl, h2l = norm_block(xl, norm2_g[i], sel_lat, res=(yl, mod, 2), mod=(mod, (3, 4)), write_x=True)
        if last:
            fl = peer(h2l, wq, kk, u, v)
        else:
            fl, *uv = peer(h2l, wq, kk, u, v, side=[(peer_u, i + 1), (peer_v, i + 1)])
        pend_l = (fl, mod, 5)
        if not last:
            xc, h2c = norm_block(xc, norm2_g[i], sel_ctx, res=(yc, mod, 2), mod=(mod, (3, 4)), write_x=True)
            pend_c = (peer(h2c, wq, kk, u, v), mod, 5)

    out = norm_block(xl, final_g, sel_lat, res=pend_l, out_dtype=F32)
    return out.reshape(b, n, d)
```
